```python
import math
import jax
import jax.numpy as jnp
from jax import lax
import numpy as np

D_MODEL = 1024
BATCH = 8
SEQ = 4096
DEPTH = 1

CHUNK = 64
Q_BLOCK = 128
MEM_LEN = 256
DA_HEADS = 8
DA_HEAD_DIM = 64
DA_V_DIM = 2 * DA_HEAD_DIM
DA_WIDTH = DA_HEADS * DA_V_DIM
SSM_WIDTH = D_MODEL
SSM_GROUP = 16
SSM_GROUPS = SSM_WIDTH // SSM_GROUP
SSM_STATE = 64
DT_MIN = 1e-3
DT_MAX = 1e-1
XA_HEADS = 4
XA_HEAD_DIM = D_MODEL // XA_HEADS
XA_WIDTH = XA_HEADS * XA_HEAD_DIM
N_BRANCH = 3
REL_BUCKETS = 32
REL_MAX_DIST = 256
FFN_HIDDEN = -(-8 * D_MODEL // (3 * 256)) * 256
Q_W = DA_HEADS * 2 * DA_HEAD_DIM
K_W = DA_HEADS * 2 * DA_HEAD_DIM
V_W = DA_WIDTH
U_W = SSM_WIDTH
XQ_W = XA_WIDTH
G_W = N_BRANCH * D_MODEL
IN_WIDTH = Q_W + K_W + V_W + U_W + XQ_W + G_W
RMS_EPS = 1e-6

kernel_name = 'hybrid_diffattn_s5_xattn_gated_block'


def rmsnorm(x, g):
    xf = x.astype(jnp.float32)
    y = xf * lax.rsqrt(jnp.mean(xf * xf, axis=-1, keepdims=True) + RMS_EPS)
    return (y * g.astype(jnp.float32)).astype(x.dtype)


def t5_bucket(rel):
    half = REL_BUCKETS // 2
    max_exact = half // 2
    ret = jnp.where(rel > 0, half, 0)
    n = jnp.abs(rel)
    nf = jnp.maximum(n, 1).astype(jnp.float32)
    large = max_exact + (jnp.log(nf / max_exact) / math.log(REL_MAX_DIST / max_exact)
                         * (half - max_exact)).astype(jnp.int32)
    large = jnp.minimum(large, half - 1)
    return ret + jnp.where(n < max_exact, n, large)


def diff_attention(q, k, v, lam, lam_init, subln_g, rel_bias):
    B, S = q.shape[0], q.shape[1]
    scale = DA_HEAD_DIM ** -0.5
    pos = jnp.arange(S, dtype=jnp.int32)
    outs = []
    for i in range(S // Q_BLOCK):
        q_lo, q_hi = i * Q_BLOCK, (i + 1) * Q_BLOCK
        qp, kp = pos[q_lo:q_hi], pos[:q_hi]
        allowed = (kp[None, :] // CHUNK) <= (qp[:, None] // CHUNK)
        bias = jnp.transpose(rel_bias[t5_bucket(kp[None, :] - qp[:, None])], (2, 0, 1)).astype(jnp.float32)
        s = jnp.einsum('bqhcd,bkhcd->bchqk', q[:, q_lo:q_hi], k[:, :q_hi]).astype(jnp.float32) * scale + bias
        s = jnp.where(allowed, s, -jnp.inf)
        p = jax.nn.softmax(s, axis=-1)
        a = p[:, 0] - lam * p[:, 1]
        outs.append(jnp.einsum('bhqk,bkhe->bqhe', a, v[:, :q_hi].astype(jnp.float32)))
    o = jnp.concatenate(outs, axis=1)
    o = o * lax.rsqrt(jnp.mean(o * o, axis=-1, keepdims=True) + RMS_EPS)
    o = o * subln_g.astype(jnp.float32) * (1.0 - lam_init)
    return o.reshape(B, S, DA_WIDTH)


def s5_ssm(u, a_re, a_im, log_dt, b_re, b_im, c_re, c_im, d_skip):
    B, S = u.shape[0], u.shape[1]
    uf = u.astype(jnp.float32)
    lam = lax.complex(a_re.astype(jnp.float32), a_im.astype(jnp.float32))
    dt = jnp.exp(log_dt.astype(jnp.float32))[:, None]
    a_bar = jnp.exp(lam * dt)
    b = lax.complex(b_re.astype(jnp.float32), b_im.astype(jnp.float32))
    b_bar = ((a_bar - 1.0) / lam)[..., None] * b
    c = lax.complex(c_re.astype(jnp.float32), c_im.astype(jnp.float32))

    def combine(e1, e2):
        a1, x1 = e1
        a2, x2 = e2
        return a1 * a2, a2 * x1 + x2

    def chunk_step(h, u_c):
        bu = jnp.einsum('gpc,blgc->blgp', b_bar, u_c)
        a_seq = jnp.broadcast_to(a_bar, bu.shape)
        a_cum, x_c = lax.associative_scan(combine, (a_seq, bu), axis=1)
        x_c = x_c + a_cum * h[:, None]
        y = jnp.einsum('gcp,blgp->blgc', c, x_c).real
        return x_c[:, -1], y

    u_chunks = jnp.moveaxis(uf.reshape(B, S // CHUNK, CHUNK, SSM_GROUPS, SSM_GROUP), 1, 0)
    h0 = jnp.zeros((B, SSM_GROUPS, SSM_STATE), jnp.complex64)
    _, ys = lax.scan(chunk_step, h0, u_chunks)
    ys = jnp.moveaxis(ys, 0, 1).reshape(B, S, SSM_WIDTH)
    return ys + d_skip.astype(jnp.float32) * uf


def memory_cross_attention(xq, mem_n, w_mem_kv):
    B, S = xq.shape[0], xq.shape[1]
    kv = mem_n @ w_mem_kv
    k = kv[..., :XA_WIDTH].reshape(B, -1, XA_HEADS, XA_HEAD_DIM)
    v = kv[..., XA_WIDTH:].reshape(B, -1, XA_HEADS, XA_HEAD_DIM)
    q = xq.reshape(B, S, XA_HEADS, XA_HEAD_DIM)
    s = jnp.einsum('bshd,bmhd->bhsm', q, k).astype(jnp.float32) * XA_HEAD_DIM ** -0.5
    p = jax.nn.softmax(s, axis=-1)
    o = jnp.einsum('bhsm,bmhd->bshd', p, v.astype(jnp.float32))
    return o.reshape(B, S, XA_WIDTH)


def setup_inputs(seed: int = 0) -> dict:
    key = jax.random.key(seed)
    ks = jax.random.split(key, 32)
    f32 = jnp.float32

    def nrm(k, shape, std):
        return jax.random.normal(k, shape, f32) * std

    L = DEPTH
    a_im = jnp.broadcast_to(jnp.pi * jnp.arange(SSM_STATE, dtype=f32), (L, SSM_GROUPS, SSM_STATE))
    return {
        'x': nrm(ks[0], (BATCH, SEQ, D_MODEL), 1.0),
        'mem': nrm(ks[1], (BATCH, MEM_LEN, D_MODEL), 1.0),
        'norm1_g': 1.0 + nrm(ks[2], (L, D_MODEL), 0.01),
        'w_in': nrm(ks[3], (L, D_MODEL, IN_WIDTH), D_MODEL ** -0.5),
        'da_lq1': nrm(ks[4], (L, DA_HEAD_DIM), 0.1),
        'da_lk1': nrm(ks[5], (L, DA_HEAD_DIM), 0.1),
        'da_lq2': nrm(ks[6], (L, DA_HEAD_DIM), 0.1),
        'da_lk2': nrm(ks[7], (L, DA_HEAD_DIM), 0.1),
        'da_subln_g': 1.0 + nrm(ks[8], (L, DA_V_DIM), 0.01),
        'rel_bias': nrm(ks[9], (REL_BUCKETS, DA_HEADS), 0.2),
        'ssm_a_re': -0.5 + nrm(ks[10], (L, SSM_GROUPS, SSM_STATE), 0.01),
        'ssm_a_im': a_im + nrm(ks[11], (L, SSM_GROUPS, SSM_STATE), 0.01),
        'ssm_log_dt': jax.random.uniform(ks[12], (L, SSM_GROUPS), f32, math.log(DT_MIN), math.log(DT_MAX)),
        'ssm_b_re': nrm(ks[13], (L, SSM_GROUPS, SSM_STATE, SSM_GROUP), (2 * SSM_GROUP) ** -0.5),
        'ssm_b_im': nrm(ks[14], (L, SSM_GROUPS, SSM_STATE, SSM_GROUP), (2 * SSM_GROUP) ** -0.5),
        'ssm_c_re': nrm(ks[15], (L, SSM_GROUPS, SSM_GROUP, SSM_STATE), (2 * SSM_STATE) ** -0.5),
        'ssm_c_im': nrm(ks[16], (L, SSM_GROUPS, SSM_GROUP, SSM_STATE), (2 * SSM_STATE) ** -0.5),
        'ssm_d': nrm(ks[17], (L, SSM_WIDTH), 0.5),
        'glu_w': nrm(ks[18], (L, SSM_WIDTH, SSM_WIDTH), SSM_WIDTH ** -0.5),
        'glu_b': nrm(ks[19], (L, SSM_WIDTH), 0.01),
        'mem_norm_g': 1.0 + nrm(ks[20], (L, D_MODEL), 0.01),
        'w_mem_kv': nrm(ks[21], (L, D_MODEL, 2 * XA_WIDTH), D_MODEL ** -0.5),
        'w_br_attn': nrm(ks[22], (L, DA_WIDTH, D_MODEL), DA_WIDTH ** -0.5),
        'w_br_ssm': nrm(ks[23], (L, SSM_WIDTH, D_MODEL), SSM_WIDTH ** -0.5),
        'w_br_xattn': nrm(ks[24], (L, XA_WIDTH, D_MODEL), XA_WIDTH ** -0.5),
        'w_out': nrm(ks[25], (L, D_MODEL, D_MODEL), D_MODEL ** -0.5),
        'norm2_g': 1.0 + nrm(ks[26], (L, D_MODEL), 0.01),
        'w_ffn_in': nrm(ks[27], (L, D_MODEL, 2 * FFN_HIDDEN), D_MODEL ** -0.5),
        'w_ffn_out': nrm(ks[28], (L, FFN_HIDDEN, D_MODEL), FFN_HIDDEN ** -0.5),
        'final_g': 1.0 + nrm(ks[29], (D_MODEL,), 0.01),
    }


def reference(x, mem, norm1_g, w_in, da_lq1, da_lk1, da_lq2, da_lk2, da_subln_g, rel_bias,
              ssm_a_re, ssm_a_im, ssm_log_dt, ssm_b_re, ssm_b_im, ssm_c_re, ssm_c_im, ssm_d,
              glu_w, glu_b, mem_norm_g, w_mem_kv, w_br_attn, w_br_ssm, w_br_xattn, w_out,
              norm2_g, w_ffn_in, w_ffn_out, final_g):
    B, S, _ = x.shape
    for layer in range(DEPTH):
        h = rmsnorm(x, norm1_g[layer])
        proj = h @ w_in[layer]
        o = 0
        q = proj[..., o:o + Q_W].reshape(B, S, DA_HEADS, 2, DA_HEAD_DIM); o += Q_W
        k = proj[..., o:o + K_W].reshape(B, S, DA_HEADS, 2, DA_HEAD_DIM); o += K_W
        v = proj[..., o:o + V_W].reshape(B, S, DA_HEADS, DA_V_DIM); o += V_W
        u = proj[..., o:o + U_W]; o += U_W
        xq = proj[..., o:o + XQ_W]; o += XQ_W
        gates = jax.nn.sigmoid(proj[..., o:o + G_W].astype(jnp.float32)).reshape(B, S, N_BRANCH, D_MODEL)

        lam_init = 0.8 - 0.6 * math.exp(-0.3 * layer)
        lam = (jnp.exp(jnp.sum(da_lq1[layer].astype(jnp.float32) * da_lk1[layer].astype(jnp.float32)))
               - jnp.exp(jnp.sum(da_lq2[layer].astype(jnp.float32) * da_lk2[layer].astype(jnp.float32)))
               + lam_init)
        y_attn = diff_attention(q, k, v, lam, lam_init, da_subln_g[layer], rel_bias).astype(x.dtype)

        y_s = s5_ssm(u, ssm_a_re[layer], ssm_a_im[layer], ssm_log_dt[layer], ssm_b_re[layer],
                     ssm_b_im[layer], ssm_c_re[layer], ssm_c_im[layer], ssm_d[layer])
        z = jax.nn.gelu(y_s).astype(x.dtype)
        y_ssm = z * jax.nn.sigmoid(z @ glu_w[layer] + glu_b[layer])

        mem_n = rmsnorm(mem, mem_norm_g[layer])
        y_x = memory_cross_attention(xq, mem_n, w_mem_kv[layer]).astype(x.dtype)

        mixed = (gates[:, :, 0] * (y_attn @ w_br_attn[layer])
                 + gates[:, :, 1] * (y_ssm @ w_br_ssm[layer])
                 + gates[:, :, 2] * (y_x @ w_br_xattn[layer]))
        x = x + mixed.astype(x.dtype) @ w_out[layer]

        h2 = rmsnorm(x, norm2_g[layer])
        gu = h2 @ w_ffn_in[layer]
        x = x + (jax.nn.silu(gu[..., :FFN_HIDDEN]) * gu[..., FFN_HIDDEN:]) @ w_ffn_out[layer]
    return rmsnorm(x, final_g)
```

```python
import math

import jax
import jax.numpy as jnp
import numpy as np
from jax import lax
from jax.experimental import pallas as pl
from jax.experimental.pallas import tpu as pltpu

F32 = jnp.float32
BF16 = jnp.bfloat16

D_MODEL = 1024
CHUNK = 64
DA_HEADS = 8
DA_HEAD_DIM = 64
DA_V_DIM = 128
SSM_GROUP = 16
SSM_GROUPS = 64
SSM_STATE = 64
XA_HEADS = 4
XA_HEAD_DIM = 256
REL_BUCKETS = 32
REL_MAX_DIST = 256
FFN_HIDDEN = 2816
RMS_EPS = 1e-6
LAM_INIT = 0.8 - 0.6 * math.exp(-0.3 * 0)

LANES = 128
SUBLANES = 8
VMEM_LIMIT_BYTES = 56 * 1024 * 1024

ATT_BLOCK = 256
PROJ_TM = 512
SSM_T = 256
SSM_LANE_BLOCKS = D_MODEL // LANES
SSM_GROUPS_PER_BLOCK = LANES // SSM_GROUP
SSM_STATE_LANES = SSM_GROUPS_PER_BLOCK * SSM_STATE
MERGE_TM = 256
FFN_TM = 512
FFN_CHUNK = 256
MASK_VALUE = -1e30
MASK_BUCKET = REL_BUCKETS


def _params(n_axes):
    return pltpu.CompilerParams(dimension_semantics=("arbitrary",) * n_axes,
                                vmem_limit_bytes=VMEM_LIMIT_BYTES)


def _const_spec(shape):
    nd = len(shape)
    return pl.BlockSpec(shape, lambda *_: (0,) * nd, pipeline_mode=pl.Buffered(1))


def _rmsnorm(xf, g):
    return xf * lax.rsqrt(jnp.mean(xf * xf, axis=-1, keepdims=True) + RMS_EPS) * g


def _t5_bucket_np(rel):
    half = REL_BUCKETS // 2
    max_exact = half // 2
    ret = np.where(rel > 0, half, 0)
    n = np.abs(rel)
    nf = np.maximum(n, 1).astype(np.float32)
    large = max_exact + (np.log(nf / np.float32(max_exact)) / np.float32(math.log(REL_MAX_DIST / max_exact))
                         * np.float32(half - max_exact)).astype(np.int32)
    large = np.minimum(large, half - 1)
    return (ret + np.where(n < max_exact, n, large)).astype(np.int32)


def _bucket_tiles():
    kk = np.arange(ATT_BLOCK)[:, None]
    qq = np.arange(ATT_BLOCK)[None, :]
    diag = _t5_bucket_np(kk - qq)
    diag = np.where((kk // CHUNK) <= (qq // CHUNK), diag, MASK_BUCKET)
    prev = _t5_bucket_np(kk - qq - ATT_BLOCK)
    assert (_t5_bucket_np(np.arange(-4 * ATT_BLOCK, -ATT_BLOCK)) == REL_BUCKETS // 2 - 1).all()
    return np.stack([diag, prev]).astype(np.int32)


def _prep_attn_kernel(relb_ref, bucket_ref, lq1_ref, lk1_ref, lq2_ref, lk2_ref, bias_ref, lam_ref):
    h = pl.program_id(0)
    bk = bucket_ref[...]
    acc = jnp.full(bk.shape, MASK_VALUE, F32)
    for b in range(REL_BUCKETS):
        acc = jnp.where(bk == b, relb_ref[b, h], acc)
    bias_ref[0] = acc - relb_ref[REL_BUCKETS // 2 - 1, h]
    lam = (jnp.exp(jnp.sum(lq1_ref[...] * lk1_ref[...], axis=-1, keepdims=True))
           - jnp.exp(jnp.sum(lq2_ref[...] * lk2_ref[...], axis=-1, keepdims=True)) + LAM_INIT)
    lam_ref[...] = jnp.broadcast_to(lam, lam_ref.shape)


def _prep_attn(rel_bias, lq1, lk1, lq2, lk2):
    buckets = jnp.asarray(_bucket_tiles())
    vec = pl.BlockSpec((1, DA_HEAD_DIM), lambda h: (0, 0))
    return pl.pallas_call(
        _prep_attn_kernel,
        grid=(DA_HEADS,),
        in_specs=[pl.BlockSpec(memory_space=pltpu.SMEM),
                  pl.BlockSpec((2, ATT_BLOCK, ATT_BLOCK), lambda h: (0, 0, 0)),
                  vec, vec, vec, vec],
        out_specs=[pl.BlockSpec((1, 2, ATT_BLOCK, ATT_BLOCK), lambda h: (h, 0, 0, 0)),
                   pl.BlockSpec((SUBLANES, LANES), lambda h: (0, 0))],
        out_shape=[jax.ShapeDtypeStruct((DA_HEADS, 2, ATT_BLOCK, ATT_BLOCK), F32),
                   jax.ShapeDtypeStruct((SUBLANES, LANES), F32)],
        compiler_params=_params(1),
        name="prep_attn",
    )(rel_bias, buckets, lq1, lk1, lq2, lk2)


def _prep_ssm_kernel(are_ref, aim_ref, ldt_ref, bre_ref, bim_ref, pr_ref, pi_ref, bbr_ref, bbi_ref):
    a_re = are_ref[...]
    a_im = aim_ref[...]
    dt = jnp.exp(ldt_ref[...])
    for n in range(1, SUBLANES + 1):
        mag = jnp.exp(a_re * dt * n)
        ang = a_im * dt * n
        pr_ref[n - 1] = mag * jnp.cos(ang)
        pi_ref[n - 1] = mag * jnp.sin(ang)
    xr = pr_ref[0] - 1.0
    xi = pi_ref[0]
    den = a_re * a_re + a_im * a_im
    cr = ((xr * a_re + xi * a_im) / den)[:, None, :]
    ci = ((xi * a_re - xr * a_im) / den)[:, None, :]
    b_re = bre_ref[...]
    b_im = bim_ref[...]
    bbr_ref[...] = cr * b_re - ci * b_im
    bbi_ref[...] = cr * b_im + ci * b_re


def _prep_ssm(a_re, a_im, log_dt, b_re, b_im):
    g, p = a_re.shape
    bt_re = jnp.transpose(b_re, (0, 2, 1))
    bt_im = jnp.transpose(b_im, (0, 2, 1))
    pow_shape = jax.ShapeDtypeStruct((SUBLANES, g, p), F32)
    bb_shape = jax.ShapeDtypeStruct((g, SSM_GROUP, p), F32)
    return pl.pallas_call(
        _prep_ssm_kernel,
        out_shape=[pow_shape, pow_shape, bb_shape, bb_shape],
        name="prep_ssm",
    )(a_re, a_im, log_dt.reshape(g, 1), bt_re, bt_im)


def _ssm_tables(pow_re, pow_im, bb_re, bb_im, c_re, c_im):
    nb, gb = SSM_LANE_BLOCKS, SSM_GROUPS_PER_BLOCK
    same_group = jnp.eye(gb, dtype=bool)[None, :, None, :, None]

    def b_block(bb):
        bb = bb.reshape(nb, gb, SSM_GROUP, 1, SSM_STATE)
        return jnp.where(same_group, bb, 0.0).reshape(nb, LANES, SSM_STATE_LANES)

    def c_block(cc):
        cc = jnp.transpose(cc.reshape(nb, gb, SSM_GROUP, SSM_STATE), (0, 1, 3, 2))
        cc = cc.reshape(nb, gb, SSM_STATE, 1, SSM_GROUP)
        return jnp.where(same_group, cc, 0.0).reshape(nb, SSM_STATE_LANES, LANES)

    b_blk = jnp.concatenate([b_block(bb_re), b_block(bb_im)], axis=2).astype(BF16)
    c_blk = jnp.concatenate([c_block(c_re), c_block(-c_im)], axis=1).astype(BF16)

    def lanes(t):
        return jnp.transpose(t.reshape(SUBLANES, nb, SSM_STATE_LANES), (1, 0, 2))

    pr, pi = lanes(pow_re), lanes(pow_im)
    row = jnp.arange(SUBLANES)[None, :, None]
    tabs = []
    for d in (1, 2, 4):
        keep = row >= d
        tabs.append(jnp.where(keep, pr[:, d - 1:d, :], 0.0))
        tabs.append(jnp.where(keep, pi[:, d - 1:d, :], 0.0))
    tabs += [pr, pi]
    coef = jnp.stack(tabs, axis=1)
    return b_blk, c_blk, coef


def _proj_kernel(x_ref, g_ref, w_ref, wvt_ref, q_ref, k_ref, u_ref, xq_ref, vt_ref):
    h = _rmsnorm(x_ref[...], g_ref[...]).astype(BF16)
    d = D_MODEL
    q_ref[...] = jnp.dot(h, w_ref[:, 0:d], preferred_element_type=F32).astype(BF16)
    k_ref[...] = jnp.dot(h, w_ref[:, d:2 * d], preferred_element_type=F32).astype(BF16)
    u_ref[...] = jnp.dot(h, w_ref[:, 2 * d:3 * d], preferred_element_type=F32)
    xq_ref[...] = jnp.dot(h, w_ref[:, 3 * d:4 * d], preferred_element_type=F32).astype(BF16)
    vt = lax.dot_general(wvt_ref[...], h, (((1,), (1,)), ((), ())), preferred_element_type=F32).astype(BF16)
    for hd in range(DA_HEADS):
        for jb in range(PROJ_TM // ATT_BLOCK):
            vt_ref[0, hd, jb] = vt[hd * DA_V_DIM:(hd + 1) * DA_V_DIM, jb * ATT_BLOCK:(jb + 1) * ATT_BLOCK]


def _proj(x2d, g, w4, wvt, batch, seq):
    n = x2d.shape[0]
    tiles_per_batch = seq // PROJ_TM
    blocks_per_tile = PROJ_TM // ATT_BLOCK
    row = pl.BlockSpec((PROJ_TM, D_MODEL), lambda i: (i, 0))
    act = lambda dt: jax.ShapeDtypeStruct((n, D_MODEL), dt)
    return pl.pallas_call(
        _proj_kernel,
        grid=(n // PROJ_TM,),
        in_specs=[row, _const_spec((1, D_MODEL)), _const_spec((D_MODEL, 4 * D_MODEL)),
                  _const_spec((D_MODEL, D_MODEL))],
        out_specs=[row, row, row, row,
                   pl.BlockSpec((1, DA_HEADS, blocks_per_tile, DA_V_DIM, ATT_BLOCK),
                                lambda i: (i // tiles_per_batch, 0, i % tiles_per_batch, 0, 0))],
        out_shape=[act(BF16), act(BF16), act(F32), act(BF16),
                   jax.ShapeDtypeStruct((batch, DA_HEADS, seq // ATT_BLOCK, DA_V_DIM, ATT_BLOCK), BF16)],
        compiler_params=_params(1),
        name="proj",
    )(x2d, g, w4, wvt)


def _attn_kernel(lam_ref, q_ref, k_ref, vt_ref, bias_ref, g_ref, o_ref, m_sc, l_sc, acc_sc):
    i = pl.program_id(2)
    blk = ATT_BLOCK
    qh = q_ref[0]
    lane = lax.broadcasted_iota(jnp.int32, qh.shape, 1)
    zero = jnp.zeros_like(qh)
    qs = (jnp.where(lane < DA_HEAD_DIM, qh, zero), jnp.where(lane >= DA_HEAD_DIM, qh, zero))

    m_sc[...] = jnp.full(m_sc.shape, MASK_VALUE, F32)
    l_sc[...] = jnp.zeros(l_sc.shape, F32)
    acc_sc[...] = jnp.zeros(acc_sc.shape, F32)

    def step(j, bias):
        off = pl.multiple_of(j * blk, blk)
        kb = k_ref[0, pl.ds(off, blk), :]
        vtb = vt_ref[0, 0, j]
        for c in range(2):
            s = lax.dot_general(kb, qs[c], (((1,), (1,)), ((), ())), preferred_element_type=F32)
            if bias is not None:
                s = s + bias
            m_old = m_sc[c]
            m_new = jnp.maximum(m_old, jnp.max(s, axis=0, keepdims=True))
            alpha = jnp.exp(m_old - m_new)
            p = jnp.exp(s - m_new)
            l_sc[c] = alpha * l_sc[c] + jnp.sum(p, axis=0, keepdims=True)
            acc_sc[c] = alpha * acc_sc[c] + jnp.dot(vtb, p.astype(BF16), preferred_element_type=F32)
            m_sc[c] = m_new

    def far_body(j, carry):
        step(j, None)
        return carry

    lax.fori_loop(0, jnp.maximum(i - 1, 0), far_body, 0)

    @pl.when(i >= 1)
    def _():
        step(i - 1, bias_ref[0, 1])

    step(i, bias_ref[0, 0])

    lam = lam_ref[0, 0]
    o = acc_sc[0] * (1.0 / l_sc[0]) - lam * (acc_sc[1] * (1.0 / l_sc[1]))
    o = o * lax.rsqrt(jnp.mean(o * o, axis=0, keepdims=True) + RMS_EPS)
    o = o * (g_ref[...] * (1.0 - LAM_INIT))
    o_ref[0] = o.T.astype(BF16)


def _attention(lam, q, k, vt, bias, g_col, batch, seq):
    blk = ATT_BLOCK
    nq = seq // blk
    return pl.pallas_call(
        _attn_kernel,
        grid=(batch, DA_HEADS, nq),
        in_specs=[pl.BlockSpec(memory_space=pltpu.SMEM),
                  pl.BlockSpec((1, blk, DA_V_DIM), lambda b, h, i: (b, i, h)),
                  pl.BlockSpec((1, seq, DA_V_DIM), lambda b, h, i: (b, 0, h)),
                  pl.BlockSpec((1, 1, nq, DA_V_DIM, blk), lambda b, h, i: (b, h, 0, 0, 0)),
                  pl.BlockSpec((1, 2, blk, blk), lambda b, h, i: (h, 0, 0, 0)),
                  pl.BlockSpec((DA_V_DIM, 1), lambda b, h, i: (0, 0))],
        out_specs=pl.BlockSpec((1, blk, DA_V_DIM), lambda b, h, i: (b, i, h)),
        out_shape=jax.ShapeDtypeStruct((batch, seq, D_MODEL), BF16),
        scratch_shapes=[pltpu.VMEM((2, 1, blk), F32), pltpu.VMEM((2, 1, blk), F32),
                        pltpu.VMEM((2, DA_V_DIM, blk), F32)],
        compiler_params=_params(3),
        name="attention",
    )(lam, q, k, vt, bias, g_col)


def _ssm_kernel(u_ref, b_ref, c_ref, coef_ref, d_ref, y_ref, h_sc):
    t = pl.program_id(2)
    ns = SSM_STATE_LANES

    @pl.when(t == 0)
    def _():
        h_sc[...] = jnp.zeros(h_sc.shape, F32)

    u = u_ref[0]
    bu = jnp.dot(u.astype(BF16), b_ref[0], preferred_element_type=F32)
    groups = SSM_T // SUBLANES
    xr = bu[:, :ns].reshape(groups, SUBLANES, ns)
    xi = bu[:, ns:].reshape(groups, SUBLANES, ns)
    for lvl, d in enumerate((1, 2, 4)):
        mr = coef_ref[0, 2 * lvl]
        mi = coef_ref[0, 2 * lvl + 1]
        sr = pltpu.roll(xr, d, axis=1)
        si = pltpu.roll(xi, d, axis=1)
        xr, xi = xr + (mr * sr - mi * si), xi + (mr * si + mi * sr)
    pr = coef_ref[0, 6]
    pi = coef_ref[0, 7]
    hr = h_sc[0]
    hi = h_sc[1]
    out_r, out_i = [], []
    for g in range(groups):
        gr = xr[g] + (pr * hr - pi * hi)
        gi = xi[g] + (pr * hi + pi * hr)
        hr = gr[SUBLANES - 1:SUBLANES]
        hi = gi[SUBLANES - 1:SUBLANES]
        out_r.append(gr)
        out_i.append(gi)
    h_sc[0] = hr
    h_sc[1] = hi
    x_all = jnp.concatenate([jnp.concatenate(out_r, axis=0), jnp.concatenate(out_i, axis=0)], axis=1)
    y = jnp.dot(x_all.astype(BF16), c_ref[0], preferred_element_type=F32)
    y_ref[0] = y + d_ref[...] * u


def _ssm(u, b_blk, c_blk, coef, d_skip, batch, seq):
    nb = SSM_LANE_BLOCKS
    act = pl.BlockSpec((1, SSM_T, LANES), lambda b, j, t: (b, t, j))
    return pl.pallas_call(
        _ssm_kernel,
        grid=(batch, nb, seq // SSM_T),
        in_specs=[act,
                  pl.BlockSpec((1, LANES, 2 * SSM_STATE_LANES), lambda b, j, t: (j, 0, 0)),
                  pl.BlockSpec((1, 2 * SSM_STATE_LANES, LANES), lambda b, j, t: (j, 0, 0)),
                  pl.BlockSpec((1, 8, SUBLANES, SSM_STATE_LANES), lambda b, j, t: (j, 0, 0, 0)),
                  pl.BlockSpec((1, LANES), lambda b, j, t: (0, j))],
        out_specs=act,
        out_shape=jax.ShapeDtypeStruct((batch, seq, D_MODEL), F32),
        scratch_shapes=[pltpu.VMEM((2, 1, SSM_STATE_LANES), F32)],
        compiler_params=_params(3),
        name="ssm",
    )(u, b_blk, c_blk, coef, d_skip)


def _mem_kv_kernel(mem_ref, g_ref, w_ref, k_ref, v_ref):
    mn = _rmsnorm(mem_ref[0], g_ref[...]).astype(BF16)
    kv = jnp.dot(mn, w_ref[...], preferred_element_type=F32)
    k_ref[0] = kv[:, :D_MODEL].astype(BF16)
    v_ref[0] = kv[:, D_MODEL:].astype(BF16)


def _mem_kv(mem, g, w):
    batch, m, _ = mem.shape
    blk = pl.BlockSpec((1, m, D_MODEL), lambda b: (b, 0, 0))
    shape = jax.ShapeDtypeStruct((batch, m, D_MODEL), BF16)
    return pl.pallas_call(
        _mem_kv_kernel,
        grid=(batch,),
        in_specs=[blk, _const_spec((1, D_MODEL)), _const_spec((D_MODEL, 2 * D_MODEL))],
        out_specs=[blk, blk],
        out_shape=[shape, shape],
        compiler_params=_params(1),
        name="mem_kv",
    )(mem, g, w)


def _gelu_tanh(x):
    return 0.5 * x * (1.0 + jnp.tanh(math.sqrt(2.0 / math.pi) * (x + 0.044715 * (x * x * x))))


def _sigmoid(x):
    return 1.0 / (1.0 + jnp.exp(-x))


def _merge_kernel(x_ref, ya_ref, ys_ref, xq_ref, km_ref, vm_ref, g1_ref, wg_ref, glu_w_ref, glu_b_ref,
                  wa_ref, ws_ref, wx_ref, wo_ref, o_ref):
    d = D_MODEL
    x = x_ref[...]
    h = _rmsnorm(x, g1_ref[...]).astype(BF16)

    mixed = _sigmoid(jnp.dot(h, wg_ref[:, 0:d], preferred_element_type=F32)) * jnp.dot(
        ya_ref[...], wa_ref[...], preferred_element_type=F32)

    z = _gelu_tanh(ys_ref[...])
    zb = z.astype(BF16)
    y_ssm = z * _sigmoid(jnp.dot(zb, glu_w_ref[...], preferred_element_type=F32) + glu_b_ref[...])
    mixed = mixed + _sigmoid(jnp.dot(h, wg_ref[:, d:2 * d], preferred_element_type=F32)) * jnp.dot(
        y_ssm.astype(BF16), ws_ref[...], preferred_element_type=F32)

    heads = []
    for hd in range(XA_HEADS):
        sl = slice(hd * XA_HEAD_DIM, (hd + 1) * XA_HEAD_DIM)
        s = lax.dot_general(xq_ref[:, sl], km_ref[0, :, sl], (((1,), (1,)), ((), ())),
                            preferred_element_type=F32) * (XA_HEAD_DIM ** -0.5)
        p = jnp.exp(s - jnp.max(s, axis=-1, keepdims=True))
        p = p * (1.0 / jnp.sum(p, axis=-1, keepdims=True))
        heads.append(jnp.dot(p.astype(BF16), vm_ref[0, :, sl], preferred_element_type=F32))
    y_x = jnp.concatenate(heads, axis=1).astype(BF16)
    mixed = mixed + _sigmoid(jnp.dot(h, wg_ref[:, 2 * d:3 * d], preferred_element_type=F32)) * jnp.dot(
        y_x, wx_ref[...], preferred_element_type=F32)

    o_ref[...] = x + jnp.dot(mixed.astype(BF16), wo_ref[...], preferred_element_type=F32)


def _merge(x2d, y_attn, y_s, xq, kmem, vmem, g1, wg, glu_w, glu_b, wa, ws, wx, wo, seq):
    n = x2d.shape[0]
    tiles_per_batch = seq // MERGE_TM
    m = kmem.shape[1]
    row = pl.BlockSpec((MERGE_TM, D_MODEL), lambda i: (i, 0))
    mem = pl.BlockSpec((1, m, D_MODEL), lambda i: (i // tiles_per_batch, 0, 0))
    sq = _const_spec((D_MODEL, D_MODEL))
    vec = _const_spec((1, D_MODEL))
    return pl.pallas_call(
        _merge_kernel,
        grid=(n // MERGE_TM,),
        in_specs=[row, row, row, row, mem, mem, vec, _const_spec((D_MODEL, 3 * D_MODEL)), sq, vec,
                  sq, sq, sq, sq],
        out_specs=row,
        out_shape=jax.ShapeDtypeStruct((n, D_MODEL), F32),
        compiler_params=_params(1),
        name="merge",
    )(x2d, y_attn, y_s, xq, kmem, vmem, g1, wg, glu_w, glu_b, wa, ws, wx, wo)


def _ffn_kernel(x_ref, g2_ref, wi_ref, wo_ref, gf_ref, o_ref):
    x = x_ref[...]
    h = _rmsnorm(x, g2_ref[...]).astype(BF16)
    acc = x
    for c in range(FFN_HIDDEN // FFN_CHUNK):
        lo = c * FFN_CHUNK
        gate = jnp.dot(h, wi_ref[:, lo:lo + FFN_CHUNK], preferred_element_type=F32)
        up = jnp.dot(h, wi_ref[:, FFN_HIDDEN + lo:FFN_HIDDEN + lo + FFN_CHUNK], preferred_element_type=F32)
        act = (gate * _sigmoid(gate) * up).astype(BF16)
        acc = acc + jnp.dot(act, wo_ref[lo:lo + FFN_CHUNK, :], preferred_element_type=F32)
    o_ref[...] = _rmsnorm(acc, gf_ref[...])


def _ffn(x2d, g2, wi, wo, gf):
    n = x2d.shape[0]
    row = pl.BlockSpec((FFN_TM, D_MODEL), lambda i: (i, 0))
    vec = _const_spec((1, D_MODEL))
    return pl.pallas_call(
        _ffn_kernel,
        grid=(n // FFN_TM,),
        in_specs=[row, vec, _const_spec((D_MODEL, 2 * FFN_HIDDEN)), _const_spec((FFN_HIDDEN, D_MODEL)), vec],
        out_specs=row,
        out_shape=jax.ShapeDtypeStruct((n, D_MODEL), F32),
        compiler_params=_params(1),
        name="ffn",
    )(x2d, g2, wi, wo, gf)


def kernel(x, mem, norm1_g, w_in, da_lq1, da_lk1, da_lq2, da_lk2, da_subln_g, rel_bias, ssm_a_re, ssm_a_im, ssm_log_dt, ssm_b_re, ssm_b_im, ssm_c_re, ssm_c_im, ssm_d, glu_w, glu_b, mem_norm_g, w_mem_kv, w_br_attn, w_br_ssm, w_br_xattn, w_out, norm2_g, w_ffn_in, w_ffn_out, final_g):
    batch, seq, d = x.shape
    depth = w_in.shape[0]
    assert depth == 1 and d == D_MODEL and seq % PROJ_TM == 0
    layer = 0
    x2d = x.reshape(batch * seq, d)
    row = lambda v: v.reshape(1, -1).astype(F32)

    w = w_in[layer]
    scale = DA_HEAD_DIM ** -0.5
    w4 = jnp.concatenate([w[:, 0:d] * scale, w[:, d:2 * d], w[:, 3 * d:4 * d], w[:, 4 * d:5 * d]],
                         axis=1).astype(BF16)
    wvt = jnp.transpose(w[:, 2 * d:3 * d]).astype(BF16)
    wg = w[:, 5 * d:8 * d].astype(BF16)

    bias, lam_tile = _prep_attn(rel_bias, row(da_lq1[layer]), row(da_lk1[layer]), row(da_lq2[layer]),
                                row(da_lk2[layer]))
    lam = lam_tile[0:1, 0:1]

    pow_re, pow_im, bb_re, bb_im = _prep_ssm(ssm_a_re[layer], ssm_a_im[layer], ssm_log_dt[layer],
                                             ssm_b_re[layer], ssm_b_im[layer])
    b_blk, c_blk, coef = _ssm_tables(pow_re, pow_im, bb_re, bb_im, ssm_c_re[layer], ssm_c_im[layer])

    q, k, u, xq, vt = _proj(x2d, row(norm1_g[layer]), w4, wvt, batch, seq)
    y_attn = _attention(lam, q.reshape(batch, seq, d), k.reshape(batch, seq, d), vt, bias,
                        da_subln_g[layer].reshape(DA_V_DIM, 1).astype(F32), batch, seq)
    y_s = _ssm(u.reshape(batch, seq, d), b_blk, c_blk, coef, row(ssm_d[layer]), batch, seq)
    kmem, vmem = _mem_kv(mem, row(mem_norm_g[layer]), w_mem_kv[layer].astype(BF16))
    x_mid = _merge(x2d, y_attn.reshape(batch * seq, d), y_s.reshape(batch * seq, d), xq, kmem, vmem,
                   row(norm1_g[layer]), wg, glu_w[layer].astype(BF16), row(glu_b[layer]),
                   w_br_attn[layer].astype(BF16), w_br_ssm[layer].astype(BF16), w_br_xattn[layer].astype(BF16),
                   w_out[layer].astype(BF16), seq)
    out = _ffn(x_mid, row(norm2_g[layer]), w_ffn_in[layer].astype(BF16), w_ffn_out[layer].astype(BF16),
               row(final_g))
    return out.reshape(batch, seq, d)
```

```python
import math

import jax
import jax.numpy as jnp
import numpy as np
from jax import lax
from jax.experimental import pallas as pl
from jax.experimental.pallas import tpu as pltpu

F32 = jnp.float32
BF16 = jnp.bfloat16

D_MODEL = 1024
CHUNK = 64
DA_HEADS = 8
DA_HEAD_DIM = 64
DA_V_DIM = 128
SSM_GROUP = 16
SSM_GROUPS = 64
SSM_STATE = 64
XA_HEADS = 4
XA_HEAD_DIM = 256
REL_BUCKETS = 32
REL_MAX_DIST = 256
FFN_HIDDEN = 2816
RMS_EPS = 1e-6
LAM_INIT = 0.8 - 0.6 * math.exp(-0.3 * 0)

LANES = 128
SUBLANES = 8
VMEM_LIMIT_BYTES = 56 * 1024 * 1024

ATT_BLOCK = 256
PROJ_TM = 512
SSM_T = 256
SSM_LANE_BLOCKS = D_MODEL // LANES
SSM_GROUPS_PER_BLOCK = LANES // SSM_GROUP
SSM_STATE_LANES = SSM_GROUPS_PER_BLOCK * SSM_STATE
MERGE_TM = 256
FFN_TM = 512
FFN_CHUNK = 256
MASK_VALUE = -1e30
MASK_BUCKET = REL_BUCKETS


def _params(n_axes):
    return pltpu.CompilerParams(dimension_semantics=("arbitrary",) * n_axes,
                                vmem_limit_bytes=VMEM_LIMIT_BYTES)


def _const_spec(shape):
    nd = len(shape)
    return pl.BlockSpec(shape, lambda *_: (0,) * nd, pipeline_mode=pl.Buffered(1))


def _rmsnorm(xf, g):
    return xf * lax.rsqrt(jnp.mean(xf * xf, axis=-1, keepdims=True) + RMS_EPS) * g


def _t5_bucket_np(rel):
    half = REL_BUCKETS // 2
    max_exact = half // 2
    ret = np.where(rel > 0, half, 0)
    n = np.abs(rel)
    nf = np.maximum(n, 1).astype(np.float32)
    large = max_exact + (np.log(nf / np.float32(max_exact)) / np.float32(math.log(REL_MAX_DIST / max_exact))
                         * np.float32(half - max_exact)).astype(np.int32)
    large = np.minimum(large, half - 1)
    return (ret + np.where(n < max_exact, n, large)).astype(np.int32)


def _bucket_tiles():
    kk = np.arange(ATT_BLOCK)[:, None]
    qq = np.arange(ATT_BLOCK)[None, :]
    diag = _t5_bucket_np(kk - qq)
    diag = np.where((kk // CHUNK) <= (qq // CHUNK), diag, MASK_BUCKET)
    prev = _t5_bucket_np(kk - qq - ATT_BLOCK)
    assert (_t5_bucket_np(np.arange(-4 * ATT_BLOCK, -ATT_BLOCK)) == REL_BUCKETS // 2 - 1).all()
    return np.stack([diag, prev]).astype(np.int32)


def _prep_attn_kernel(relb_ref, bucket_ref, lq1_ref, lk1_ref, lq2_ref, lk2_ref, bias_ref, lam_ref):
    h = pl.program_id(0)
    bk = bucket_ref[...]
    acc = jnp.full(bk.shape, MASK_VALUE, F32)
    for b in range(REL_BUCKETS):
        acc = jnp.where(bk == b, relb_ref[b, h], acc)
    bias_ref[0] = acc - relb_ref[REL_BUCKETS // 2 - 1, h]
    lam = (jnp.exp(jnp.sum(lq1_ref[...] * lk1_ref[...], axis=-1, keepdims=True))
           - jnp.exp(jnp.sum(lq2_ref[...] * lk2_ref[...], axis=-1, keepdims=True)) + LAM_INIT)
    lam_ref[...] = jnp.broadcast_to(lam, lam_ref.shape)


def _prep_attn(rel_bias, lq1, lk1, lq2, lk2):
    buckets = jnp.asarray(_bucket_tiles())
    vec = pl.BlockSpec((1, DA_HEAD_DIM), lambda h: (0, 0))
    return pl.pallas_call(
        _prep_attn_kernel,
        grid=(DA_HEADS,),
        in_specs=[pl.BlockSpec(memory_space=pltpu.SMEM),
                  pl.BlockSpec((2, ATT_BLOCK, ATT_BLOCK), lambda h: (0, 0, 0)),
                  vec, vec, vec, vec],
        out_specs=[pl.BlockSpec((1, 2, ATT_BLOCK, ATT_BLOCK), lambda h: (h, 0, 0, 0)),
                   pl.BlockSpec((SUBLANES, LANES), lambda h: (0, 0))],
        out_shape=[jax.ShapeDtypeStruct((DA_HEADS, 2, ATT_BLOCK, ATT_BLOCK), F32),
                   jax.ShapeDtypeStruct((SUBLANES, LANES), F32)],
        compiler_params=_params(1),
        name="prep_attn",
    )(rel_bias, buckets, lq1, lk1, lq2, lk2)


def _prep_ssm_kernel(are_ref, aim_ref, ldt_ref, bre_ref, bim_ref, pr_ref, pi_ref, bbr_ref, bbi_ref):
    a_re = are_ref[...]
    a_im = aim_ref[...]
    dt = jnp.exp(ldt_ref[...])
    for n in range(1, SUBLANES + 1):
        mag = jnp.exp(a_re * dt * n)
        ang = a_im * dt * n
        pr_ref[n - 1] = mag * jnp.cos(ang)
        pi_ref[n - 1] = mag * jnp.sin(ang)
    xr = pr_ref[0] - 1.0
    xi = pi_ref[0]
    den = a_re * a_re + a_im * a_im
    cr = ((xr * a_re + xi * a_im) / den)[:, None, :]
    ci = ((xi * a_re - xr * a_im) / den)[:, None, :]
    b_re = bre_ref[...]
    b_im = bim_ref[...]
    bbr_ref[...] = cr * b_re - ci * b_im
    bbi_ref[...] = cr * b_im + ci * b_re


def _prep_ssm(a_re, a_im, log_dt, b_re, b_im):
    g, p = a_re.shape
    bt_re = jnp.transpose(b_re, (0, 2, 1))
    bt_im = jnp.transpose(b_im, (0, 2, 1))
    pow_shape = jax.ShapeDtypeStruct((SUBLANES, g, p), F32)
    bb_shape = jax.ShapeDtypeStruct((g, SSM_GROUP, p), F32)
    return pl.pallas_call(
        _prep_ssm_kernel,
        out_shape=[pow_shape, pow_shape, bb_shape, bb_shape],
        name="prep_ssm",
    )(a_re, a_im, log_dt.reshape(g, 1), bt_re, bt_im)


def _ssm_tables(pow_re, pow_im, bb_re, bb_im, c_re, c_im):
    nb, gb = SSM_LANE_BLOCKS, SSM_GROUPS_PER_BLOCK
    same_group = jnp.eye(gb, dtype=bool)[None, :, None, :, None]

    def b_block(bb):
        bb = bb.reshape(nb, gb, SSM_GROUP, 1, SSM_STATE)
        return jnp.where(same_group, bb, 0.0).reshape(nb, LANES, SSM_STATE_LANES)

    def c_block(cc):
        cc = jnp.transpose(cc.reshape(nb, gb, SSM_GROUP, SSM_STATE), (0, 1, 3, 2))
        cc = cc.reshape(nb, gb, SSM_STATE, 1, SSM_GROUP)
        return jnp.where(same_group, cc, 0.0).reshape(nb, SSM_STATE_LANES, LANES)

    b_blk = jnp.concatenate([b_block(bb_re), b_block(bb_im)], axis=2).astype(BF16)
    c_blk = jnp.concatenate([c_block(c_re), c_block(-c_im)], axis=1).astype(BF16)

    def lanes(t):
        return jnp.transpose(t.reshape(SUBLANES, nb, SSM_STATE_LANES), (1, 0, 2))

    pr, pi = lanes(pow_re), lanes(pow_im)
    row = jnp.arange(SUBLANES)[None, :, None]
    tabs = []
    for d in (1, 2, 4):
        keep = row >= d
        tabs.append(jnp.where(keep, pr[:, d - 1:d, :], 0.0))
        tabs.append(jnp.where(keep, pi[:, d - 1:d, :], 0.0))
    tabs += [pr, pi]
    coef = jnp.stack(tabs, axis=1)
    return b_blk, c_blk, coef


def _proj_kernel(x_ref, g_ref, w_ref, wvt_ref, q_ref, k_ref, u_ref, xq_ref, vt_ref):
    h = _rmsnorm(x_ref[...], g_ref[...]).astype(BF16)
    d = D_MODEL
    q_ref[...] = jnp.dot(h, w_ref[:, 0:d], preferred_element_type=F32).astype(BF16)
    k_ref[...] = jnp.dot(h, w_ref[:, d:2 * d], preferred_element_type=F32).astype(BF16)
    u_ref[...] = jnp.dot(h, w_ref[:, 2 * d:3 * d], preferred_element_type=F32)
    xq_ref[...] = jnp.dot(h, w_ref[:, 3 * d:4 * d], preferred_element_type=F32).astype(BF16)
    vt = lax.dot_general(wvt_ref[...], h, (((1,), (1,)), ((), ())), preferred_element_type=F32).astype(BF16)
    for hd in range(DA_HEADS):
        for jb in range(PROJ_TM // ATT_BLOCK):
            vt_ref[0, hd, jb] = vt[hd * DA_V_DIM:(hd + 1) * DA_V_DIM, jb * ATT_BLOCK:(jb + 1) * ATT_BLOCK]


def _proj(x2d, g, w4, wvt, batch, seq):
    n = x2d.shape[0]
    tiles_per_batch = seq // PROJ_TM
    blocks_per_tile = PROJ_TM // ATT_BLOCK
    row = pl.BlockSpec((PROJ_TM, D_MODEL), lambda i: (i, 0))
    act = lambda dt: jax.ShapeDtypeStruct((n, D_MODEL), dt)
    return pl.pallas_call(
        _proj_kernel,
        grid=(n // PROJ_TM,),
        in_specs=[row, _const_spec((1, D_MODEL)), _const_spec((D_MODEL, 4 * D_MODEL)),
                  _const_spec((D_MODEL, D_MODEL))],
        out_specs=[row, row, row, row,
                   pl.BlockSpec((1, DA_HEADS, blocks_per_tile, DA_V_DIM, ATT_BLOCK),
                                lambda i: (i // tiles_per_batch, 0, i % tiles_per_batch, 0, 0))],
        out_shape=[act(BF16), act(BF16), act(F32), act(BF16),
                   jax.ShapeDtypeStruct((batch, DA_HEADS, seq // ATT_BLOCK, DA_V_DIM, ATT_BLOCK), BF16)],
        compiler_params=_params(1),
        name="proj",
    )(x2d, g, w4, wvt)


def _attn_kernel(lam_ref, q_ref, k_ref, vt_ref, bias_ref, g_ref, o_ref, m_sc, l_sc, acc_sc, s_sc, p_sc, a_sc):
    i = pl.program_id(2)
    blk = ATT_BLOCK
    qh = q_ref[0]
    lane = lax.broadcasted_iota(jnp.int32, qh.shape, 1)
    zero = jnp.zeros_like(qh)
    qs = (jnp.where(lane < DA_HEAD_DIM, qh, zero), jnp.where(lane >= DA_HEAD_DIM, qh, zero))

    m_sc[...] = jnp.full(m_sc.shape, MASK_VALUE, F32)
    l_sc[...] = jnp.zeros(l_sc.shape, F32)
    acc_sc[...] = jnp.zeros(acc_sc.shape, F32)
    p_sc[...] = jnp.zeros(p_sc.shape, BF16)
    a_sc[...] = jnp.ones(a_sc.shape, F32)

    def scores(j, slot):
        off = pl.multiple_of(j * blk, blk)
        kb = k_ref[0, pl.ds(off, blk), :]
        for c in range(2):
            s_sc[slot, c] = lax.dot_general(kb, qs[c], (((1,), (1,)), ((), ())), preferred_element_type=F32)

    def values(j):
        vtb = vt_ref[0, 0, j]
        for c in range(2):
            acc_sc[c] = a_sc[c] * acc_sc[c] + jnp.dot(vtb, p_sc[c], preferred_element_type=F32)

    def softmax(slot, bias):
        for c in range(2):
            s = s_sc[slot, c]
            if bias is not None:
                s = s + bias
            m_old = m_sc[c]
            m_new = jnp.maximum(m_old, jnp.max(s, axis=0, keepdims=True))
            alpha = jnp.exp(m_old - m_new)
            p = jnp.exp(s - m_new)
            l_sc[c] = alpha * l_sc[c] + jnp.sum(p, axis=0, keepdims=True)
            m_sc[c] = m_new
            a_sc[c] = alpha
            p_sc[c] = p.astype(BF16)

    scores(0, 0)

    def far_body(t, carry):
        values(jnp.maximum(t - 1, 0))
        softmax(t & 1, None)
        scores(t + 1, (t + 1) & 1)
        return carry

    lax.fori_loop(0, jnp.maximum(i - 1, 0), far_body, 0)

    @pl.when(i >= 1)
    def _():
        values(jnp.maximum(i - 2, 0))
        softmax((i - 1) & 1, bias_ref[0, 1])
        scores(i, i & 1)

    values(jnp.maximum(i - 1, 0))
    softmax(i & 1, bias_ref[0, 0])
    values(i)

    lam = lam_ref[0, 0]
    o = acc_sc[0] * (1.0 / l_sc[0]) - lam * (acc_sc[1] * (1.0 / l_sc[1]))
    o = o * lax.rsqrt(jnp.mean(o * o, axis=0, keepdims=True) + RMS_EPS)
    o = o * (g_ref[...] * (1.0 - LAM_INIT))
    o_ref[0] = o.T.astype(BF16)


def _attention(lam, q, k, vt, bias, g_col, batch, seq):
    blk = ATT_BLOCK
    nq = seq // blk
    return pl.pallas_call(
        _attn_kernel,
        grid=(batch, DA_HEADS, nq),
        in_specs=[pl.BlockSpec(memory_space=pltpu.SMEM),
                  pl.BlockSpec((1, blk, DA_V_DIM), lambda b, h, i: (b, i, h)),
                  pl.BlockSpec((1, seq, DA_V_DIM), lambda b, h, i: (b, 0, h)),
                  pl.BlockSpec((1, 1, nq, DA_V_DIM, blk), lambda b, h, i: (b, h, 0, 0, 0)),
                  pl.BlockSpec((1, 2, blk, blk), lambda b, h, i: (h, 0, 0, 0)),
                  pl.BlockSpec((DA_V_DIM, 1), lambda b, h, i: (0, 0))],
        out_specs=pl.BlockSpec((1, blk, DA_V_DIM), lambda b, h, i: (b, i, h)),
        out_shape=jax.ShapeDtypeStruct((batch, seq, D_MODEL), BF16),
        scratch_shapes=[pltpu.VMEM((2, 1, blk), F32), pltpu.VMEM((2, 1, blk), F32),
                        pltpu.VMEM((2, DA_V_DIM, blk), F32),
                        pltpu.VMEM((2, 2, blk, blk), F32),
                        pltpu.VMEM((2, blk, blk), BF16),
                        pltpu.VMEM((2, 1, blk), F32)],
        compiler_params=_params(3),
        name="attention",
    )(lam, q, k, vt, bias, g_col)


def _ssm_kernel(u_ref, b_ref, c_ref, coef_ref, d_ref, y_ref, h_sc):
    t = pl.program_id(2)
    ns = SSM_STATE_LANES

    @pl.when(t == 0)
    def _():
        h_sc[...] = jnp.zeros(h_sc.shape, F32)

    u = u_ref[0]
    bu = jnp.dot(u.astype(BF16), b_ref[0], preferred_element_type=F32)
    groups = SSM_T // SUBLANES
    xr = bu[:, :ns].reshape(groups, SUBLANES, ns)
    xi = bu[:, ns:].reshape(groups, SUBLANES, ns)
    for lvl, d in enumerate((1, 2, 4)):
        mr = coef_ref[0, 2 * lvl]
        mi = coef_ref[0, 2 * lvl + 1]
        sr = pltpu.roll(xr, d, axis=1)
        si = pltpu.roll(xi, d, axis=1)
        xr, xi = xr + (mr * sr - mi * si), xi + (mr * si + mi * sr)
    pr = coef_ref[0, 6]
    pi = coef_ref[0, 7]
    hr = h_sc[0]
    hi = h_sc[1]
    out_r, out_i = [], []
    for g in range(groups):
        gr = xr[g] + (pr * hr - pi * hi)
        gi = xi[g] + (pr * hi + pi * hr)
        hr = gr[SUBLANES - 1:SUBLANES]
        hi = gi[SUBLANES - 1:SUBLANES]
        out_r.append(gr)
        out_i.append(gi)
    h_sc[0] = hr
    h_sc[1] = hi
    x_all = jnp.concatenate([jnp.concatenate(out_r, axis=0), jnp.concatenate(out_i, axis=0)], axis=1)
    y = jnp.dot(x_all.astype(BF16), c_ref[0], preferred_element_type=F32)
    y_ref[0] = y + d_ref[...] * u


def _ssm(u, b_blk, c_blk, coef, d_skip, batch, seq):
    nb = SSM_LANE_BLOCKS
    act = pl.BlockSpec((1, SSM_T, LANES), lambda b, j, t: (b, t, j))
    return pl.pallas_call(
        _ssm_kernel,
        grid=(batch, nb, seq // SSM_T),
        in_specs=[act,
                  pl.BlockSpec((1, LANES, 2 * SSM_STATE_LANES), lambda b, j, t: (j, 0, 0)),
                  pl.BlockSpec((1, 2 * SSM_STATE_LANES, LANES), lambda b, j, t: (j, 0, 0)),
                  pl.BlockSpec((1, 8, SUBLANES, SSM_STATE_LANES), lambda b, j, t: (j, 0, 0, 0)),
                  pl.BlockSpec((1, LANES), lambda b, j, t: (0, j))],
        out_specs=act,
        out_shape=jax.ShapeDtypeStruct((batch, seq, D_MODEL), F32),
        scratch_shapes=[pltpu.VMEM((2, 1, SSM_STATE_LANES), F32)],
        compiler_params=_params(3),
        name="ssm",
    )(u, b_blk, c_blk, coef, d_skip)


def _mem_kv_kernel(mem_ref, g_ref, w_ref, k_ref, v_ref):
    mn = _rmsnorm(mem_ref[0], g_ref[...]).astype(BF16)
    kv = jnp.dot(mn, w_ref[...], preferred_element_type=F32)
    k_ref[0] = kv[:, :D_MODEL].astype(BF16)
    v_ref[0] = kv[:, D_MODEL:].astype(BF16)


def _mem_kv(mem, g, w):
    batch, m, _ = mem.shape
    blk = pl.BlockSpec((1, m, D_MODEL), lambda b: (b, 0, 0))
    shape = jax.ShapeDtypeStruct((batch, m, D_MODEL), BF16)
    return pl.pallas_call(
        _mem_kv_kernel,
        grid=(batch,),
        in_specs=[blk, _const_spec((1, D_MODEL)), _const_spec((D_MODEL, 2 * D_MODEL))],
        out_specs=[blk, blk],
        out_shape=[shape, shape],
        compiler_params=_params(1),
        name="mem_kv",
    )(mem, g, w)


def _gelu_tanh(x):
    return 0.5 * x * (1.0 + jnp.tanh(math.sqrt(2.0 / math.pi) * (x + 0.044715 * (x * x * x))))


def _sigmoid(x):
    return 1.0 / (1.0 + jnp.exp(-x))


def _merge_kernel(x_ref, ya_ref, ys_ref, xq_ref, km_ref, vm_ref, g1_ref, wg_ref, glu_w_ref, glu_b_ref,
                  wa_ref, ws_ref, wx_ref, wo_ref, o_ref):
    d = D_MODEL
    x = x_ref[...]
    h = _rmsnorm(x, g1_ref[...]).astype(BF16)

    mixed = _sigmoid(jnp.dot(h, wg_ref[:, 0:d], preferred_element_type=F32)) * jnp.dot(
        ya_ref[...], wa_ref[...], preferred_element_type=F32)

    z = _gelu_tanh(ys_ref[...])
    zb = z.astype(BF16)
    y_ssm = z * _sigmoid(jnp.dot(zb, glu_w_ref[...], preferred_element_type=F32) + glu_b_ref[...])
    mixed = mixed + _sigmoid(jnp.dot(h, wg_ref[:, d:2 * d], preferred_element_type=F32)) * jnp.dot(
        y_ssm.astype(BF16), ws_ref[...], preferred_element_type=F32)

    heads = []
    for hd in range(XA_HEADS):
        sl = slice(hd * XA_HEAD_DIM, (hd + 1) * XA_HEAD_DIM)
        s = lax.dot_general(xq_ref[:, sl], km_ref[0, :, sl], (((1,), (1,)), ((), ())),
                            preferred_element_type=F32) * (XA_HEAD_DIM ** -0.5)
        p = jnp.exp(s - jnp.max(s, axis=-1, keepdims=True))
        p = p * (1.0 / jnp.sum(p, axis=-1, keepdims=True))
        heads.append(jnp.dot(p.astype(BF16), vm_ref[0, :, sl], preferred_element_type=F32))
    y_x = jnp.concatenate(heads, axis=1).astype(BF16)
    mixed = mixed + _sigmoid(jnp.dot(h, wg_ref[:, 2 * d:3 * d], preferred_element_type=F32)) * jnp.dot(
        y_x, wx_ref[...], preferred_element_type=F32)

    o_ref[...] = x + jnp.dot(mixed.astype(BF16), wo_ref[...], preferred_element_type=F32)


def _merge(x2d, y_attn, y_s, xq, kmem, vmem, g1, wg, glu_w, glu_b, wa, ws, wx, wo, seq):
    n = x2d.shape[0]
    tiles_per_batch = seq // MERGE_TM
    m = kmem.shape[1]
    row = pl.BlockSpec((MERGE_TM, D_MODEL), lambda i: (i, 0))
    mem = pl.BlockSpec((1, m, D_MODEL), lambda i: (i // tiles_per_batch, 0, 0))
    sq = _const_spec((D_MODEL, D_MODEL))
    vec = _const_spec((1, D_MODEL))
    return pl.pallas_call(
        _merge_kernel,
        grid=(n // MERGE_TM,),
        in_specs=[row, row, row, row, mem, mem, vec, _const_spec((D_MODEL, 3 * D_MODEL)), sq, vec,
                  sq, sq, sq, sq],
        out_specs=row,
        out_shape=jax.ShapeDtypeStruct((n, D_MODEL), F32),
        compiler_params=_params(1),
        name="merge",
    )(x2d, y_attn, y_s, xq, kmem, vmem, g1, wg, glu_w, glu_b, wa, ws, wx, wo)


def _ffn_kernel(x_ref, g2_ref, wi_ref, wo_ref, gf_ref, o_ref):
    x = x_ref[...]
    h = _rmsnorm(x, g2_ref[...]).astype(BF16)
    acc = x
    for c in range(FFN_HIDDEN // FFN_CHUNK):
        lo = c * FFN_CHUNK
        gate = jnp.dot(h, wi_ref[:, lo:lo + FFN_CHUNK], preferred_element_type=F32)
        up = jnp.dot(h, wi_ref[:, FFN_HIDDEN + lo:FFN_HIDDEN + lo + FFN_CHUNK], preferred_element_type=F32)
        act = (gate * _sigmoid(gate) * up).astype(BF16)
        acc = acc + jnp.dot(act, wo_ref[lo:lo + FFN_CHUNK, :], preferred_element_type=F32)
    o_ref[...] = _rmsnorm(acc, gf_ref[...])


def _ffn(x2d, g2, wi, wo, gf):
    n = x2d.shape[0]
    row = pl.BlockSpec((FFN_TM, D_MODEL), lambda i: (i, 0))
    vec = _const_spec((1, D_MODEL))
    return pl.pallas_call(
        _ffn_kernel,
        grid=(n // FFN_TM,),
        in_specs=[row, vec, _const_spec((D_MODEL, 2 * FFN_HIDDEN)), _const_spec((FFN_HIDDEN, D_MODEL)), vec],
        out_specs=row,
        out_shape=jax.ShapeDtypeStruct((n, D_MODEL), F32),
        compiler_params=_params(1),
        name="ffn",
    )(x2d, g2, wi, wo, gf)


def kernel(x, mem, norm1_g, w_in, da_lq1, da_lk1, da_lq2, da_lk2, da_subln_g, rel_bias, ssm_a_re, ssm_a_im, ssm_log_dt, ssm_b_re, ssm_b_im, ssm_c_re, ssm_c_im, ssm_d, glu_w, glu_b, mem_norm_g, w_mem_kv, w_br_attn, w_br_ssm, w_br_xattn, w_out, norm2_g, w_ffn_in, w_ffn_out, final_g):
    batch, seq, d = x.shape
    depth = w_in.shape[0]
    assert depth == 1 and d == D_MODEL and seq % PROJ_TM == 0
    layer = 0
    x2d = x.reshape(batch * seq, d)
    row = lambda v: v.reshape(1, -1).astype(F32)

    w = w_in[layer]
    scale = DA_HEAD_DIM ** -0.5
    w4 = jnp.concatenate([w[:, 0:d] * scale, w[:, d:2 * d], w[:, 3 * d:4 * d], w[:, 4 * d:5 * d]],
                         axis=1).astype(BF16)
    wvt = jnp.transpose(w[:, 2 * d:3 * d]).astype(BF16)
    wg = w[:, 5 * d:8 * d].astype(BF16)

    bias, lam_tile = _prep_attn(rel_bias, row(da_lq1[layer]), row(da_lk1[layer]), row(da_lq2[layer]),
                                row(da_lk2[layer]))
    lam = lam_tile[0:1, 0:1]

    pow_re, pow_im, bb_re, bb_im = _prep_ssm(ssm_a_re[layer], ssm_a_im[layer], ssm_log_dt[layer],
                                             ssm_b_re[layer], ssm_b_im[layer])
    b_blk, c_blk, coef = _ssm_tables(pow_re, pow_im, bb_re, bb_im, ssm_c_re[layer], ssm_c_im[layer])

    q, k, u, xq, vt = _proj(x2d, row(norm1_g[layer]), w4, wvt, batch, seq)
    y_attn = _attention(lam, q.reshape(batch, seq, d), k.reshape(batch, seq, d), vt, bias,
                        da_subln_g[layer].reshape(DA_V_DIM, 1).astype(F32), batch, seq)
    y_s = _ssm(u.reshape(batch, seq, d), b_blk, c_blk, coef, row(ssm_d[layer]), batch, seq)
    kmem, vmem = _mem_kv(mem, row(mem_norm_g[layer]), w_mem_kv[layer].astype(BF16))
    x_mid = _merge(x2d, y_attn.reshape(batch * seq, d), y_s.reshape(batch * seq, d), xq, kmem, vmem,
                   row(norm1_g[layer]), wg, glu_w[layer].astype(BF16), row(glu_b[layer]),
                   w_br_attn[layer].astype(BF16), w_br_ssm[layer].astype(BF16), w_br_xattn[layer].astype(BF16),
                   w_out[layer].astype(BF16), seq)
    out = _ffn(x_mid, row(norm2_g[layer]), w_ffn_in[layer].astype(BF16), w_ffn_out[layer].astype(BF16),
               row(final_g))
    return out.reshape(batch, seq, d)
```

```python
import math

import jax
import jax.numpy as jnp
import numpy as np
from jax import lax
from jax.experimental import pallas as pl
from jax.experimental.pallas import tpu as pltpu

F32 = jnp.float32
BF16 = jnp.bfloat16

D_MODEL = 1024
CHUNK = 64
DA_HEADS = 8
DA_HEAD_DIM = 64
DA_V_DIM = 128
SSM_GROUP = 16
SSM_GROUPS = 64
SSM_STATE = 64
XA_HEADS = 4
XA_HEAD_DIM = 256
REL_BUCKETS = 32
REL_MAX_DIST = 256
FFN_HIDDEN = 2816
RMS_EPS = 1e-6
LAM_INIT = 0.8 - 0.6 * math.exp(-0.3 * 0)

LANES = 128
SUBLANES = 8
VMEM_LIMIT_BYTES = 56 * 1024 * 1024

ATT_BQ = 512
ATT_BK = 512
ATT_NEAR = ATT_BQ // ATT_BK + 1
ATT_V_ROWS = DA_V_DIM + 16
LOG2E = math.log2(math.e)
PROJ_TM = 512
SSM_T = 256
SSM_LANE_BLOCKS = D_MODEL // LANES
SSM_GROUPS_PER_BLOCK = LANES // SSM_GROUP
SSM_STATE_LANES = SSM_GROUPS_PER_BLOCK * SSM_STATE
MERGE_TM = 256
FFN_TM = 512
FFN_CHUNK = 256
MASK_VALUE = -1e30
MASK_BUCKET = REL_BUCKETS


def _params(n_axes, flags=None):
    return pltpu.CompilerParams(dimension_semantics=("arbitrary",) * n_axes,
                                vmem_limit_bytes=VMEM_LIMIT_BYTES, flags=flags)


def _const_spec(shape):
    nd = len(shape)
    return pl.BlockSpec(shape, lambda *_: (0,) * nd, pipeline_mode=pl.Buffered(1))


def _rmsnorm(xf, g):
    return xf * lax.rsqrt(jnp.mean(xf * xf, axis=-1, keepdims=True) + RMS_EPS) * g


def _t5_bucket_np(rel):
    half = REL_BUCKETS // 2
    max_exact = half // 2
    ret = np.where(rel > 0, half, 0)
    n = np.abs(rel)
    nf = np.maximum(n, 1).astype(np.float32)
    large = max_exact + (np.log(nf / np.float32(max_exact)) / np.float32(math.log(REL_MAX_DIST / max_exact))
                         * np.float32(half - max_exact)).astype(np.int32)
    large = np.minimum(large, half - 1)
    return (ret + np.where(n < max_exact, n, large)).astype(np.int32)


def _bucket_tiles():
    kk = np.arange(ATT_BK)[:, None]
    qq = np.arange(ATT_BQ)[None, :]
    tiles = []
    for n in range(ATT_NEAR):
        key = kk + (n - 1) * ATT_BK
        tile = _t5_bucket_np(key - qq)
        tiles.append(np.where((key // CHUNK) <= (qq // CHUNK), tile, MASK_BUCKET))
    nearest_far = (-2 * ATT_BK + ATT_BK - 1) - 0
    assert (_t5_bucket_np(np.arange(-8 * ATT_BQ, nearest_far + 1)) == REL_BUCKETS // 2 - 1).all()
    return np.stack(tiles).astype(np.int32)


def _prep_attn_kernel(relb_ref, bucket_ref, lq1_ref, lk1_ref, lq2_ref, lk2_ref, bias_ref, lam_ref):
    h = pl.program_id(0)
    bk = bucket_ref[...]
    acc = jnp.full(bk.shape, MASK_VALUE, F32)
    for b in range(REL_BUCKETS):
        acc = jnp.where(bk == b, relb_ref[b, h], acc)
    bias_ref[0] = (acc - relb_ref[REL_BUCKETS // 2 - 1, h]) * LOG2E
    lam = (jnp.exp(jnp.sum(lq1_ref[...] * lk1_ref[...], axis=-1, keepdims=True))
           - jnp.exp(jnp.sum(lq2_ref[...] * lk2_ref[...], axis=-1, keepdims=True)) + LAM_INIT)
    lam_ref[...] = jnp.broadcast_to(lam, lam_ref.shape)


def _prep_attn(rel_bias, lq1, lk1, lq2, lk2):
    buckets = jnp.asarray(_bucket_tiles())
    vec = pl.BlockSpec((1, DA_HEAD_DIM), lambda h: (0, 0))
    return pl.pallas_call(
        _prep_attn_kernel,
        grid=(DA_HEADS,),
        in_specs=[pl.BlockSpec(memory_space=pltpu.SMEM),
                  pl.BlockSpec((ATT_NEAR, ATT_BK, ATT_BQ), lambda h: (0, 0, 0)),
                  vec, vec, vec, vec],
        out_specs=[pl.BlockSpec((1, ATT_NEAR, ATT_BK, ATT_BQ), lambda h: (h, 0, 0, 0)),
                   pl.BlockSpec((SUBLANES, LANES), lambda h: (0, 0))],
        out_shape=[jax.ShapeDtypeStruct((DA_HEADS, ATT_NEAR, ATT_BK, ATT_BQ), F32),
                   jax.ShapeDtypeStruct((SUBLANES, LANES), F32)],
        compiler_params=_params(1),
        name="prep_attn",
    )(rel_bias, buckets, lq1, lk1, lq2, lk2)


def _prep_ssm_kernel(are_ref, aim_ref, ldt_ref, bre_ref, bim_ref, pr_ref, pi_ref, bbr_ref, bbi_ref):
    a_re = are_ref[...]
    a_im = aim_ref[...]
    dt = jnp.exp(ldt_ref[...])
    for n in range(1, SUBLANES + 1):
        mag = jnp.exp(a_re * dt * n)
        ang = a_im * dt * n
        pr_ref[n - 1] = mag * jnp.cos(ang)
        pi_ref[n - 1] = mag * jnp.sin(ang)
    xr = pr_ref[0] - 1.0
    xi = pi_ref[0]
    den = a_re * a_re + a_im * a_im
    cr = ((xr * a_re + xi * a_im) / den)[:, None, :]
    ci = ((xi * a_re - xr * a_im) / den)[:, None, :]
    b_re = bre_ref[...]
    b_im = bim_ref[...]
    bbr_ref[...] = cr * b_re - ci * b_im
    bbi_ref[...] = cr * b_im + ci * b_re


def _prep_ssm(a_re, a_im, log_dt, b_re, b_im):
    g, p = a_re.shape
    bt_re = jnp.transpose(b_re, (0, 2, 1))
    bt_im = jnp.transpose(b_im, (0, 2, 1))
    pow_shape = jax.ShapeDtypeStruct((SUBLANES, g, p), F32)
    bb_shape = jax.ShapeDtypeStruct((g, SSM_GROUP, p), F32)
    return pl.pallas_call(
        _prep_ssm_kernel,
        out_shape=[pow_shape, pow_shape, bb_shape, bb_shape],
        name="prep_ssm",
    )(a_re, a_im, log_dt.reshape(g, 1), bt_re, bt_im)


def _ssm_tables(pow_re, pow_im, bb_re, bb_im, c_re, c_im):
    nb, gb = SSM_LANE_BLOCKS, SSM_GROUPS_PER_BLOCK
    same_group = jnp.eye(gb, dtype=bool)[None, :, None, :, None]

    def b_block(bb):
        bb = bb.reshape(nb, gb, SSM_GROUP, 1, SSM_STATE)
        return jnp.where(same_group, bb, 0.0).reshape(nb, LANES, SSM_STATE_LANES)

    def c_block(cc):
        cc = jnp.transpose(cc.reshape(nb, gb, SSM_GROUP, SSM_STATE), (0, 1, 3, 2))
        cc = cc.reshape(nb, gb, SSM_STATE, 1, SSM_GROUP)
        return jnp.where(same_group, cc, 0.0).reshape(nb, SSM_STATE_LANES, LANES)

    b_blk = jnp.concatenate([b_block(bb_re), b_block(bb_im)], axis=2).astype(BF16)
    c_blk = jnp.concatenate([c_block(c_re), c_block(-c_im)], axis=1).astype(BF16)

    def lanes(t):
        return jnp.transpose(t.reshape(SUBLANES, nb, SSM_STATE_LANES), (1, 0, 2))

    pr, pi = lanes(pow_re), lanes(pow_im)
    row = jnp.arange(SUBLANES)[None, :, None]
    tabs = []
    for d in (1, 2, 4):
        keep = row >= d
        tabs.append(jnp.where(keep, pr[:, d - 1:d, :], 0.0))
        tabs.append(jnp.where(keep, pi[:, d - 1:d, :], 0.0))
    tabs += [pr, pi]
    coef = jnp.stack(tabs, axis=1)
    return b_blk, c_blk, coef


def _proj_kernel(x_ref, g_ref, w_ref, wvt_ref, q_ref, k_ref, u_ref, xq_ref, vt_ref):
    h = _rmsnorm(x_ref[...], g_ref[...]).astype(BF16)
    d = D_MODEL
    q_ref[...] = jnp.dot(h, w_ref[:, 0:d], preferred_element_type=F32).astype(BF16)
    k_ref[...] = jnp.dot(h, w_ref[:, d:2 * d], preferred_element_type=F32).astype(BF16)
    u_ref[...] = jnp.dot(h, w_ref[:, 2 * d:3 * d], preferred_element_type=F32)
    xq_ref[...] = jnp.dot(h, w_ref[:, 3 * d:4 * d], preferred_element_type=F32).astype(BF16)
    vt = lax.dot_general(wvt_ref[...], h, (((1,), (1,)), ((), ())), preferred_element_type=F32).astype(BF16)
    for hd in range(DA_HEADS):
        for jb in range(PROJ_TM // ATT_BK):
            vt_ref[0, hd, jb, 0:DA_V_DIM, :] = vt[hd * DA_V_DIM:(hd + 1) * DA_V_DIM,
                                                   jb * ATT_BK:(jb + 1) * ATT_BK]
            tail_row = lax.broadcasted_iota(jnp.int32, (ATT_V_ROWS - DA_V_DIM, ATT_BK), 0)
            vt_ref[0, hd, jb, DA_V_DIM:ATT_V_ROWS, :] = jnp.where(tail_row == 0, 1.0, 0.0).astype(BF16)


def _proj(x2d, g, w4, wvt, batch, seq):
    n = x2d.shape[0]
    tiles_per_batch = seq // PROJ_TM
    blocks_per_tile = PROJ_TM // ATT_BK
    row = pl.BlockSpec((PROJ_TM, D_MODEL), lambda i: (i, 0))
    act = lambda dt: jax.ShapeDtypeStruct((n, D_MODEL), dt)
    return pl.pallas_call(
        _proj_kernel,
        grid=(n // PROJ_TM,),
        in_specs=[row, _const_spec((1, D_MODEL)), _const_spec((D_MODEL, 4 * D_MODEL)),
                  _const_spec((D_MODEL, D_MODEL))],
        out_specs=[row, row, row, row,
                   pl.BlockSpec((1, DA_HEADS, blocks_per_tile, ATT_V_ROWS, ATT_BK),
                                lambda i: (i // tiles_per_batch, 0, i % tiles_per_batch, 0, 0))],
        out_shape=[act(BF16), act(BF16), act(F32), act(BF16),
                   jax.ShapeDtypeStruct((batch, DA_HEADS, seq // ATT_BK, ATT_V_ROWS, ATT_BK), BF16)],
        compiler_params=_params(1),
        name="proj",
    )(x2d, g, w4, wvt)


def _attn_kernel(lam_ref, q_ref, k_ref, vt_ref, bias_ref, g_ref, o_ref, m_sc, acc_sc, s_sc, bm_sc, p_sc, a_sc):
    i = pl.program_id(2)
    ratio = ATT_BQ // ATT_BK
    first_near = ratio * i - 1
    last = first_near + ATT_NEAR - 1
    qh = q_ref[0]
    lane = lax.broadcasted_iota(jnp.int32, qh.shape, 1)
    zero = jnp.zeros_like(qh)
    qs = (jnp.where(lane < DA_HEAD_DIM, qh, zero), jnp.where(lane >= DA_HEAD_DIM, qh, zero))

    m_sc[...] = jnp.full(m_sc.shape, MASK_VALUE, F32)
    acc_sc[...] = jnp.zeros(acc_sc.shape, F32)
    p_sc[...] = jnp.zeros(p_sc.shape, BF16)
    a_sc[...] = jnp.ones(a_sc.shape, F32)

    def score_matmul(j):
        off = pl.multiple_of(j * ATT_BK, ATT_BK)
        kb = k_ref[0, pl.ds(off, ATT_BK), :]
        return [lax.dot_general(kb, qs[c], (((1,), (1,)), ((), ())), preferred_element_type=F32)
                for c in range(2)]

    def score_store(raw, tile):
        for c in range(2):
            s = raw[c] if tile is None else raw[c] + bias_ref[0, tile]
            s_sc[c] = s
            bm_sc[c] = jnp.max(s, axis=0, keepdims=True)

    def values(j):
        vtb = vt_ref[0, 0, j]
        for c in range(2):
            acc_sc[c] = a_sc[c] * acc_sc[c] + jnp.dot(vtb, p_sc[c], preferred_element_type=F32)

    def softmax():
        for c in range(2):
            m_old = m_sc[c]
            m_new = jnp.maximum(m_old, bm_sc[c])
            m_sc[c] = m_new
            a_sc[c] = jnp.exp2(m_old - m_new)
            p_sc[c] = jnp.exp2(s_sc[c] - m_new).astype(BF16)

    def step(t, next_tile):
        values(jnp.maximum(t - 1, 0))
        softmax()
        raw = score_matmul(t + 1)
        score_store(raw, next_tile)

    first_blocks = [i0 for i0 in range(ATT_NEAR) if 0 <= 1 - ratio * i0 < ATT_NEAR]
    for i0 in first_blocks:
        pl.when(i == i0)(lambda i0=i0: score_store(score_matmul(0), 1 - ratio * i0))
    pl.when(i > first_blocks[-1])(lambda: score_store(score_matmul(0), None))

    def far_body(t, carry):
        step(t, None)
        return carry

    lax.fori_loop(0, jnp.maximum(first_near - 1, 0), far_body, 0)

    for n in range(ATT_NEAR):
        i_min = -((n - 2) // ratio) if n < 2 else 0
        if i_min > 0:
            pl.when(i >= i_min)(lambda n=n: step(first_near - 1 + n, n))
        else:
            step(first_near - 1 + n, n)

    values(jnp.maximum(last - 1, 0))
    softmax()
    values(last)

    lam = lam_ref[0, 0]
    num = [acc_sc[c, 0:DA_V_DIM, :] for c in range(2)]
    den = [acc_sc[c, DA_V_DIM:DA_V_DIM + 1, :] for c in range(2)]
    o = num[0] * (1.0 / den[0]) - lam * (num[1] * (1.0 / den[1]))
    o = o * lax.rsqrt(jnp.mean(o * o, axis=0, keepdims=True) + RMS_EPS)
    o = o * (g_ref[...] * (1.0 - LAM_INIT))
    o_ref[0] = o.T.astype(BF16)


def _attention(lam, q, k, vt, bias, g_col, batch, seq):
    bq, bk = ATT_BQ, ATT_BK
    return pl.pallas_call(
        _attn_kernel,
        grid=(batch, DA_HEADS, seq // bq),
        in_specs=[pl.BlockSpec(memory_space=pltpu.SMEM),
                  pl.BlockSpec((1, bq, DA_V_DIM), lambda b, h, i: (b, i, h)),
                  pl.BlockSpec((1, seq, DA_V_DIM), lambda b, h, i: (b, 0, h)),
                  pl.BlockSpec((1, 1, seq // bk, ATT_V_ROWS, bk), lambda b, h, i: (b, h, 0, 0, 0)),
                  pl.BlockSpec((1, ATT_NEAR, bk, bq), lambda b, h, i: (h, 0, 0, 0)),
                  pl.BlockSpec((DA_V_DIM, 1), lambda b, h, i: (0, 0))],
        out_specs=pl.BlockSpec((1, bq, DA_V_DIM), lambda b, h, i: (b, i, h)),
        out_shape=jax.ShapeDtypeStruct((batch, seq, D_MODEL), BF16),
        scratch_shapes=[pltpu.VMEM((2, 1, bq), F32),
                        pltpu.VMEM((2, ATT_V_ROWS, bq), F32),
                        pltpu.VMEM((2, bk, bq), F32),
                        pltpu.VMEM((2, 1, bq), F32),
                        pltpu.VMEM((2, bk, bq), BF16),
                        pltpu.VMEM((2, 1, bq), F32)],
        compiler_params=_params(3),
        name="attention",
    )(lam, q, k, vt, bias, g_col)


def _ssm_kernel(u_ref, b_ref, c_ref, coef_ref, d_ref, y_ref, h_sc):
    t = pl.program_id(2)
    ns = SSM_STATE_LANES

    @pl.when(t == 0)
    def _():
        h_sc[...] = jnp.zeros(h_sc.shape, F32)

    u = u_ref[0]
    bu = jnp.dot(u.astype(BF16), b_ref[0], preferred_element_type=F32)
    groups = SSM_T // SUBLANES
    xr = bu[:, :ns].reshape(groups, SUBLANES, ns)
    xi = bu[:, ns:].reshape(groups, SUBLANES, ns)
    for lvl, d in enumerate((1, 2, 4)):
        mr = coef_ref[0, 2 * lvl]
        mi = coef_ref[0, 2 * lvl + 1]
        sr = pltpu.roll(xr, d, axis=1)
        si = pltpu.roll(xi, d, axis=1)
        xr, xi = xr + (mr * sr - mi * si), xi + (mr * si + mi * sr)
    pr = coef_ref[0, 6]
    pi = coef_ref[0, 7]
    hr = h_sc[0]
    hi = h_sc[1]
    out_r, out_i = [], []
    for g in range(groups):
        gr = xr[g] + (pr * hr - pi * hi)
        gi = xi[g] + (pr * hi + pi * hr)
        hr = gr[SUBLANES - 1:SUBLANES]
        hi = gi[SUBLANES - 1:SUBLANES]
        out_r.append(gr)
        out_i.append(gi)
    h_sc[0] = hr
    h_sc[1] = hi
    x_all = jnp.concatenate([jnp.concatenate(out_r, axis=0), jnp.concatenate(out_i, axis=0)], axis=1)
    y = jnp.dot(x_all.astype(BF16), c_ref[0], preferred_element_type=F32)
    y_ref[0] = y + d_ref[...] * u


def _ssm(u, b_blk, c_blk, coef, d_skip, batch, seq):
    nb = SSM_LANE_BLOCKS
    act = pl.BlockSpec((1, SSM_T, LANES), lambda b, j, t: (b, t, j))
    return pl.pallas_call(
        _ssm_kernel,
        grid=(batch, nb, seq // SSM_T),
        in_specs=[act,
                  pl.BlockSpec((1, LANES, 2 * SSM_STATE_LANES), lambda b, j, t: (j, 0, 0)),
                  pl.BlockSpec((1, 2 * SSM_STATE_LANES, LANES), lambda b, j, t: (j, 0, 0)),
                  pl.BlockSpec((1, 8, SUBLANES, SSM_STATE_LANES), lambda b, j, t: (j, 0, 0, 0)),
                  pl.BlockSpec((1, LANES), lambda b, j, t: (0, j))],
        out_specs=act,
        out_shape=jax.ShapeDtypeStruct((batch, seq, D_MODEL), F32),
        scratch_shapes=[pltpu.VMEM((2, 1, SSM_STATE_LANES), F32)],
        compiler_params=_params(3),
        name="ssm",
    )(u, b_blk, c_blk, coef, d_skip)


def _mem_kv_kernel(mem_ref, g_ref, w_ref, k_ref, v_ref):
    mn = _rmsnorm(mem_ref[0], g_ref[...]).astype(BF16)
    kv = jnp.dot(mn, w_ref[...], preferred_element_type=F32)
    k_ref[0] = kv[:, :D_MODEL].astype(BF16)
    v_ref[0] = kv[:, D_MODEL:].astype(BF16)


def _mem_kv(mem, g, w):
    batch, m, _ = mem.shape
    blk = pl.BlockSpec((1, m, D_MODEL), lambda b: (b, 0, 0))
    shape = jax.ShapeDtypeStruct((batch, m, D_MODEL), BF16)
    return pl.pallas_call(
        _mem_kv_kernel,
        grid=(batch,),
        in_specs=[blk, _const_spec((1, D_MODEL)), _const_spec((D_MODEL, 2 * D_MODEL))],
        out_specs=[blk, blk],
        out_shape=[shape, shape],
        compiler_params=_params(1),
        name="mem_kv",
    )(mem, g, w)


def _gelu_tanh(x):
    return 0.5 * x * (1.0 + jnp.tanh(math.sqrt(2.0 / math.pi) * (x + 0.044715 * (x * x * x))))


def _sigmoid(x):
    return 1.0 / (1.0 + jnp.exp(-x))


def _merge_kernel(x_ref, ya_ref, ys_ref, xq_ref, km_ref, vm_ref, g1_ref, wg_ref, glu_w_ref, glu_b_ref,
                  wa_ref, ws_ref, wx_ref, wo_ref, o_ref):
    d = D_MODEL
    x = x_ref[...]
    h = _rmsnorm(x, g1_ref[...]).astype(BF16)

    mixed = _sigmoid(jnp.dot(h, wg_ref[:, 0:d], preferred_element_type=F32)) * jnp.dot(
        ya_ref[...], wa_ref[...], preferred_element_type=F32)

    z = _gelu_tanh(ys_ref[...])
    zb = z.astype(BF16)
    y_ssm = z * _sigmoid(jnp.dot(zb, glu_w_ref[...], preferred_element_type=F32) + glu_b_ref[...])
    mixed = mixed + _sigmoid(jnp.dot(h, wg_ref[:, d:2 * d], preferred_element_type=F32)) * jnp.dot(
        y_ssm.astype(BF16), ws_ref[...], preferred_element_type=F32)

    heads = []
    for hd in range(XA_HEADS):
        sl = slice(hd * XA_HEAD_DIM, (hd + 1) * XA_HEAD_DIM)
        s = lax.dot_general(xq_ref[:, sl], km_ref[0, :, sl], (((1,), (1,)), ((), ())),
                            preferred_element_type=F32) * (XA_HEAD_DIM ** -0.5)
        p = jnp.exp(s - jnp.max(s, axis=-1, keepdims=True))
        p = p * (1.0 / jnp.sum(p, axis=-1, keepdims=True))
        heads.append(jnp.dot(p.astype(BF16), vm_ref[0, :, sl], preferred_element_type=F32))
    y_x = jnp.concatenate(heads, axis=1).astype(BF16)
    mixed = mixed + _sigmoid(jnp.dot(h, wg_ref[:, 2 * d:3 * d], preferred_element_type=F32)) * jnp.dot(
        y_x, wx_ref[...], preferred_element_type=F32)

    o_ref[...] = x + jnp.dot(mixed.astype(BF16), wo_ref[...], preferred_element_type=F32)


def _merge(x2d, y_attn, y_s, xq, kmem, vmem, g1, wg, glu_w, glu_b, wa, ws, wx, wo, seq):
    n = x2d.shape[0]
    tiles_per_batch = seq // MERGE_TM
    m = kmem.shape[1]
    row = pl.BlockSpec((MERGE_TM, D_MODEL), lambda i: (i, 0))
    mem = pl.BlockSpec((1, m, D_MODEL), lambda i: (i // tiles_per_batch, 0, 0))
    sq = _const_spec((D_MODEL, D_MODEL))
    vec = _const_spec((1, D_MODEL))
    return pl.pallas_call(
        _merge_kernel,
        grid=(n // MERGE_TM,),
        in_specs=[row, row, row, row, mem, mem, vec, _const_spec((D_MODEL, 3 * D_MODEL)), sq, vec,
                  sq, sq, sq, sq],
        out_specs=row,
        out_shape=jax.ShapeDtypeStruct((n, D_MODEL), F32),
        compiler_params=_params(1),
        name="merge",
    )(x2d, y_attn, y_s, xq, kmem, vmem, g1, wg, glu_w, glu_b, wa, ws, wx, wo)


def _ffn_kernel(x_ref, g2_ref, wi_ref, wo_ref, gf_ref, o_ref):
    x = x_ref[...]
    h = _rmsnorm(x, g2_ref[...]).astype(BF16)
    acc = x
    for c in range(FFN_HIDDEN // FFN_CHUNK):
        lo = c * FFN_CHUNK
        gate = jnp.dot(h, wi_ref[:, lo:lo + FFN_CHUNK], preferred_element_type=F32)
        up = jnp.dot(h, wi_ref[:, FFN_HIDDEN + lo:FFN_HIDDEN + lo + FFN_CHUNK], preferred_element_type=F32)
        act = (gate * _sigmoid(gate) * up).astype(BF16)
        acc = acc + jnp.dot(act, wo_ref[lo:lo + FFN_CHUNK, :], preferred_element_type=F32)
    o_ref[...] = _rmsnorm(acc, gf_ref[...])


def _ffn(x2d, g2, wi, wo, gf):
    n = x2d.shape[0]
    row = pl.BlockSpec((FFN_TM, D_MODEL), lambda i: (i, 0))
    vec = _const_spec((1, D_MODEL))
    return pl.pallas_call(
        _ffn_kernel,
        grid=(n // FFN_TM,),
        in_specs=[row, vec, _const_spec((D_MODEL, 2 * FFN_HIDDEN)), _const_spec((FFN_HIDDEN, D_MODEL)), vec],
        out_specs=row,
        out_shape=jax.ShapeDtypeStruct((n, D_MODEL), F32),
        compiler_params=_params(1),
        name="ffn",
    )(x2d, g2, wi, wo, gf)


def kernel(x, mem, norm1_g, w_in, da_lq1, da_lk1, da_lq2, da_lk2, da_subln_g, rel_bias, ssm_a_re, ssm_a_im, ssm_log_dt, ssm_b_re, ssm_b_im, ssm_c_re, ssm_c_im, ssm_d, glu_w, glu_b, mem_norm_g, w_mem_kv, w_br_attn, w_br_ssm, w_br_xattn, w_out, norm2_g, w_ffn_in, w_ffn_out, final_g):
    batch, seq, d = x.shape
    depth = w_in.shape[0]
    assert depth == 1 and d == D_MODEL and seq % PROJ_TM == 0
    layer = 0
    x2d = x.reshape(batch * seq, d)
    row = lambda v: v.reshape(1, -1).astype(F32)

    w = w_in[layer]
    scale = DA_HEAD_DIM ** -0.5 * LOG2E
    w4 = jnp.concatenate([w[:, 0:d] * scale, w[:, d:2 * d], w[:, 3 * d:4 * d], w[:, 4 * d:5 * d]],
                         axis=1).astype(BF16)
    wvt = jnp.transpose(w[:, 2 * d:3 * d]).astype(BF16)
    wg = w[:, 5 * d:8 * d].astype(BF16)

    bias, lam_tile = _prep_attn(rel_bias, row(da_lq1[layer]), row(da_lk1[layer]), row(da_lq2[layer]),
                                row(da_lk2[layer]))
    lam = lam_tile[0:1, 0:1]

    pow_re, pow_im, bb_re, bb_im = _prep_ssm(ssm_a_re[layer], ssm_a_im[layer], ssm_log_dt[layer],
                                             ssm_b_re[layer], ssm_b_im[layer])
    b_blk, c_blk, coef = _ssm_tables(pow_re, pow_im, bb_re, bb_im, ssm_c_re[layer], ssm_c_im[layer])

    q, k, u, xq, vt = _proj(x2d, row(norm1_g[layer]), w4, wvt, batch, seq)
    y_attn = _attention(lam, q.reshape(batch, seq, d), k.reshape(batch, seq, d), vt, bias,
                        da_subln_g[layer].reshape(DA_V_DIM, 1).astype(F32), batch, seq)
    y_s = _ssm(u.reshape(batch, seq, d), b_blk, c_blk, coef, row(ssm_d[layer]), batch, seq)
    kmem, vmem = _mem_kv(mem, row(mem_norm_g[layer]), w_mem_kv[layer].astype(BF16))
    x_mid = _merge(x2d, y_attn.reshape(batch * seq, d), y_s.reshape(batch * seq, d), xq, kmem, vmem,
                   row(norm1_g[layer]), wg, glu_w[layer].astype(BF16), row(glu_b[layer]),
                   w_br_attn[layer].astype(BF16), w_br_ssm[layer].astype(BF16), w_br_xattn[layer].astype(BF16),
                   w_out[layer].astype(BF16), seq)
    out = _ffn(x_mid, row(norm2_g[layer]), w_ffn_in[layer].astype(BF16), w_ffn_out[layer].astype(BF16),
               row(final_g))
    return out.reshape(batch, seq, d)
```

```python
import math

import jax
import jax.numpy as jnp
import numpy as np
from jax import lax
from jax.experimental import pallas as pl
from jax.experimental.pallas import tpu as pltpu

F32 = jnp.float32
BF16 = jnp.bfloat16

D_MODEL = 1024
CHUNK = 64
DA_HEADS = 8
DA_HEAD_DIM = 64
DA_V_DIM = 128
SSM_GROUP = 16
SSM_GROUPS = 64
SSM_STATE = 64
XA_HEADS = 4
XA_HEAD_DIM = 256
REL_BUCKETS = 32
REL_MAX_DIST = 256
FFN_HIDDEN = 2816
RMS_EPS = 1e-6
LAM_INIT = 0.8 - 0.6 * math.exp(-0.3 * 0)

LANES = 128
SUBLANES = 8
VMEM_LIMIT_BYTES = 56 * 1024 * 1024

ATT_BQ = 512
ATT_BK = 512
ATT_NEAR = ATT_BQ // ATT_BK + 1
ATT_V_ROWS = DA_V_DIM + 16
LOG2E = math.log2(math.e)
PROJ_TM = 512
SSM_T = 512
SSM_ROWS = 128
SSM_LANE_BLOCKS = D_MODEL // LANES
SSM_HALVES = 2
SSM_HALF_LANES = LANES // SSM_HALVES
SSM_GROUPS_PER_HALF = SSM_HALF_LANES // SSM_GROUP
SSM_HALF_STATES = SSM_GROUPS_PER_HALF * SSM_STATE
MERGE_TM = 256
FFN_TM = 512
FFN_CHUNK = 256
MASK_VALUE = -1e30
MASK_BUCKET = REL_BUCKETS


def _params(n_axes, flags=None):
    return pltpu.CompilerParams(dimension_semantics=("arbitrary",) * n_axes,
                                vmem_limit_bytes=VMEM_LIMIT_BYTES, flags=flags)


def _const_spec(shape):
    nd = len(shape)
    return pl.BlockSpec(shape, lambda *_: (0,) * nd, pipeline_mode=pl.Buffered(1))


def _rmsnorm(xf, g):
    return xf * lax.rsqrt(jnp.mean(xf * xf, axis=-1, keepdims=True) + RMS_EPS) * g


def _t5_bucket_np(rel):
    half = REL_BUCKETS // 2
    max_exact = half // 2
    ret = np.where(rel > 0, half, 0)
    n = np.abs(rel)
    nf = np.maximum(n, 1).astype(np.float32)
    large = max_exact + (np.log(nf / np.float32(max_exact)) / np.float32(math.log(REL_MAX_DIST / max_exact))
                         * np.float32(half - max_exact)).astype(np.int32)
    large = np.minimum(large, half - 1)
    return (ret + np.where(n < max_exact, n, large)).astype(np.int32)


def _bucket_tiles():
    kk = np.arange(ATT_BK)[:, None]
    qq = np.arange(ATT_BQ)[None, :]
    tiles = []
    for n in range(ATT_NEAR):
        key = kk + (n - 1) * ATT_BK
        tile = _t5_bucket_np(key - qq)
        tiles.append(np.where((key // CHUNK) <= (qq // CHUNK), tile, MASK_BUCKET))
    nearest_far = (-2 * ATT_BK + ATT_BK - 1) - 0
    assert (_t5_bucket_np(np.arange(-8 * ATT_BQ, nearest_far + 1)) == REL_BUCKETS // 2 - 1).all()
    return np.stack(tiles).astype(np.int32)


def _prep_attn_kernel(relb_ref, bucket_ref, lq1_ref, lk1_ref, lq2_ref, lk2_ref, bias_ref, lam_ref):
    h = pl.program_id(0)
    bk = bucket_ref[...]
    acc = jnp.full(bk.shape, MASK_VALUE, F32)
    for b in range(REL_BUCKETS):
        acc = jnp.where(bk == b, relb_ref[b, h], acc)
    bias_ref[0] = (acc - relb_ref[REL_BUCKETS // 2 - 1, h]) * LOG2E
    lam = (jnp.exp(jnp.sum(lq1_ref[...] * lk1_ref[...], axis=-1, keepdims=True))
           - jnp.exp(jnp.sum(lq2_ref[...] * lk2_ref[...], axis=-1, keepdims=True)) + LAM_INIT)
    lam_ref[...] = jnp.broadcast_to(lam, lam_ref.shape)


def _prep_attn(rel_bias, lq1, lk1, lq2, lk2):
    buckets = jnp.asarray(_bucket_tiles())
    vec = pl.BlockSpec((1, DA_HEAD_DIM), lambda h: (0, 0))
    return pl.pallas_call(
        _prep_attn_kernel,
        grid=(DA_HEADS,),
        in_specs=[pl.BlockSpec(memory_space=pltpu.SMEM),
                  pl.BlockSpec((ATT_NEAR, ATT_BK, ATT_BQ), lambda h: (0, 0, 0)),
                  vec, vec, vec, vec],
        out_specs=[pl.BlockSpec((1, ATT_NEAR, ATT_BK, ATT_BQ), lambda h: (h, 0, 0, 0)),
                   pl.BlockSpec((SUBLANES, LANES), lambda h: (0, 0))],
        out_shape=[jax.ShapeDtypeStruct((DA_HEADS, ATT_NEAR, ATT_BK, ATT_BQ), F32),
                   jax.ShapeDtypeStruct((SUBLANES, LANES), F32)],
        compiler_params=_params(1),
        name="prep_attn",
    )(rel_bias, buckets, lq1, lk1, lq2, lk2)


def _prep_ssm_kernel(are_ref, aim_ref, ldt_ref, bre_ref, bim_ref, pr_ref, pi_ref, wr_ref, wi_ref):
    a_re = are_ref[...]
    a_im = aim_ref[...]
    dt = jnp.exp(ldt_ref[...])
    for n in range(1, SUBLANES + 1):
        mag = jnp.exp(a_re * dt * n)
        ang = a_im * dt * n
        pr_ref[n - 1] = mag * jnp.cos(ang)
        pi_ref[n - 1] = mag * jnp.sin(ang)
    xr = pr_ref[0] - 1.0
    xi = pi_ref[0]
    den = a_re * a_re + a_im * a_im
    cr = ((xr * a_re + xi * a_im) / den)[:, None, :]
    ci = ((xi * a_re - xr * a_im) / den)[:, None, :]
    b_re = bre_ref[...]
    b_im = bim_ref[...]
    bbr = cr * b_re - ci * b_im
    bbi = cr * b_im + ci * b_re
    wr_ref[0] = bbr
    wi_ref[0] = bbi
    for d in range(1, SUBLANES):
        ar = pr_ref[d - 1][:, None, :]
        ai = pi_ref[d - 1][:, None, :]
        wr_ref[d] = ar * bbr - ai * bbi
        wi_ref[d] = ar * bbi + ai * bbr


def _prep_ssm(a_re, a_im, log_dt, b_re, b_im):
    g, p = a_re.shape
    bt_re = jnp.transpose(b_re, (0, 2, 1))
    bt_im = jnp.transpose(b_im, (0, 2, 1))
    pow_shape = jax.ShapeDtypeStruct((SUBLANES, g, p), F32)
    w_shape = jax.ShapeDtypeStruct((SUBLANES, g, SSM_GROUP, p), F32)
    return pl.pallas_call(
        _prep_ssm_kernel,
        out_shape=[pow_shape, pow_shape, w_shape, w_shape],
        name="prep_ssm",
    )(a_re, a_im, log_dt.reshape(g, 1), bt_re, bt_im)


def _ssm_tables(pow_re, pow_im, w_re, w_im, c_re, c_im):
    nb, nh, gh = SSM_LANE_BLOCKS, SSM_HALVES, SSM_GROUPS_PER_HALF
    eye_g = jnp.eye(gh, dtype=bool)
    eye_h = jnp.eye(nh, dtype=bool)

    def w_block(w):
        w = jnp.transpose(w.reshape(SUBLANES, nb, nh, gh, SSM_GROUP, 1, SSM_STATE), (1, 2, 0, 3, 4, 5, 6))
        same_group = eye_g[None, None, None, :, None, :, None]
        return jnp.where(same_group, w, 0.0).reshape(nb, nh, SUBLANES * SSM_HALF_LANES, SSM_HALF_STATES)

    def c_block(cc):
        cc = jnp.transpose(cc.reshape(nb, nh, gh, SSM_GROUP, SSM_STATE), (0, 1, 2, 4, 3))
        cc = cc.reshape(nb, nh, gh, SSM_STATE, 1, 1, SSM_GROUP)
        same = (eye_h[None, :, None, None, :, None, None] & eye_g[None, None, :, None, None, :, None])
        return jnp.where(same, cc, 0.0).reshape(nb, nh, SSM_HALF_STATES, LANES)

    w_blk = jnp.concatenate([w_block(w_re), w_block(w_im)], axis=3).astype(BF16)
    c_blk = jnp.concatenate([c_block(c_re), c_block(-c_im)], axis=2).astype(BF16)

    def lanes(t):
        return jnp.transpose(t.reshape(SUBLANES, nb, nh, SSM_HALF_STATES), (1, 2, 0, 3))

    coef = jnp.stack([lanes(pow_re), lanes(pow_im)], axis=2)
    return w_blk, c_blk, coef


def _proj_kernel(x_ref, g_ref, w_ref, wvt_ref, q_ref, k_ref, u_ref, xq_ref, vt_ref):
    h = _rmsnorm(x_ref[...], g_ref[...]).astype(BF16)
    d = D_MODEL
    q_ref[...] = jnp.dot(h, w_ref[:, 0:d], preferred_element_type=F32).astype(BF16)
    k_ref[...] = jnp.dot(h, w_ref[:, d:2 * d], preferred_element_type=F32).astype(BF16)
    u_ref[...] = jnp.dot(h, w_ref[:, 2 * d:3 * d], preferred_element_type=F32)
    xq_ref[...] = jnp.dot(h, w_ref[:, 3 * d:4 * d], preferred_element_type=F32).astype(BF16)
    vt = lax.dot_general(wvt_ref[...], h, (((1,), (1,)), ((), ())), preferred_element_type=F32).astype(BF16)
    for hd in range(DA_HEADS):
        for jb in range(PROJ_TM // ATT_BK):
            vt_ref[0, hd, jb, 0:DA_V_DIM, :] = vt[hd * DA_V_DIM:(hd + 1) * DA_V_DIM,
                                                   jb * ATT_BK:(jb + 1) * ATT_BK]
            tail_row = lax.broadcasted_iota(jnp.int32, (ATT_V_ROWS - DA_V_DIM, ATT_BK), 0)
            vt_ref[0, hd, jb, DA_V_DIM:ATT_V_ROWS, :] = jnp.where(tail_row == 0, 1.0, 0.0).astype(BF16)


def _proj(x2d, g, w4, wvt, batch, seq):
    n = x2d.shape[0]
    tiles_per_batch = seq // PROJ_TM
    blocks_per_tile = PROJ_TM // ATT_BK
    row = pl.BlockSpec((PROJ_TM, D_MODEL), lambda i: (i, 0))
    act = lambda dt: jax.ShapeDtypeStruct((n, D_MODEL), dt)
    return pl.pallas_call(
        _proj_kernel,
        grid=(n // PROJ_TM,),
        in_specs=[row, _const_spec((1, D_MODEL)), _const_spec((D_MODEL, 4 * D_MODEL)),
                  _const_spec((D_MODEL, D_MODEL))],
        out_specs=[row, row, row, row,
                   pl.BlockSpec((1, DA_HEADS, blocks_per_tile, ATT_V_ROWS, ATT_BK),
                                lambda i: (i // tiles_per_batch, 0, i % tiles_per_batch, 0, 0))],
        out_shape=[act(BF16), act(BF16), act(F32), act(BF16),
                   jax.ShapeDtypeStruct((batch, DA_HEADS, seq // ATT_BK, ATT_V_ROWS, ATT_BK), BF16)],
        compiler_params=_params(1),
        name="proj",
    )(x2d, g, w4, wvt)


def _attn_kernel(lam_ref, q_ref, k_ref, vt_ref, bias_ref, g_ref, o_ref, m_sc, acc_sc, s_sc, bm_sc, p_sc, a_sc):
    i = pl.program_id(2)
    ratio = ATT_BQ // ATT_BK
    first_near = ratio * i - 1
    last = first_near + ATT_NEAR - 1
    qh = q_ref[0]
    lane = lax.broadcasted_iota(jnp.int32, qh.shape, 1)
    zero = jnp.zeros_like(qh)
    qs = (jnp.where(lane < DA_HEAD_DIM, qh, zero), jnp.where(lane >= DA_HEAD_DIM, qh, zero))

    m_sc[...] = jnp.full(m_sc.shape, MASK_VALUE, F32)
    acc_sc[...] = jnp.zeros(acc_sc.shape, F32)
    p_sc[...] = jnp.zeros(p_sc.shape, BF16)
    a_sc[...] = jnp.ones(a_sc.shape, F32)

    def score_matmul(j):
        off = pl.multiple_of(j * ATT_BK, ATT_BK)
        kb = k_ref[0, pl.ds(off, ATT_BK), :]
        return [lax.dot_general(kb, qs[c], (((1,), (1,)), ((), ())), preferred_element_type=F32)
                for c in range(2)]

    def score_store(raw, tile):
        for c in range(2):
            s = raw[c] if tile is None else raw[c] + bias_ref[0, tile]
            s_sc[c] = s
            bm_sc[c] = jnp.max(s, axis=0, keepdims=True)

    def values(j):
        vtb = vt_ref[0, 0, j]
        for c in range(2):
            acc_sc[c] = a_sc[c] * acc_sc[c] + jnp.dot(vtb, p_sc[c], preferred_element_type=F32)

    def softmax():
        for c in range(2):
            m_old = m_sc[c]
            m_new = jnp.maximum(m_old, bm_sc[c])
            m_sc[c] = m_new
            a_sc[c] = jnp.exp2(m_old - m_new)
            p_sc[c] = jnp.exp2(s_sc[c] - m_new).astype(BF16)

    def step(t, next_tile):
        values(jnp.maximum(t - 1, 0))
        softmax()
        raw = score_matmul(t + 1)
        score_store(raw, next_tile)

    first_blocks = [i0 for i0 in range(ATT_NEAR) if 0 <= 1 - ratio * i0 < ATT_NEAR]
    for i0 in first_blocks:
        pl.when(i == i0)(lambda i0=i0: score_store(score_matmul(0), 1 - ratio * i0))
    pl.when(i > first_blocks[-1])(lambda: score_store(score_matmul(0), None))

    def far_body(t, carry):
        step(t, None)
        return carry

    lax.fori_loop(0, jnp.maximum(first_near - 1, 0), far_body, 0)

    for n in range(ATT_NEAR):
        i_min = -((n - 2) // ratio) if n < 2 else 0
        if i_min > 0:
            pl.when(i >= i_min)(lambda n=n: step(first_near - 1 + n, n))
        else:
            step(first_near - 1 + n, n)

    values(jnp.maximum(last - 1, 0))
    softmax()
    values(last)

    lam = lam_ref[0, 0]
    num = [acc_sc[c, 0:DA_V_DIM, :] for c in range(2)]
    den = [acc_sc[c, DA_V_DIM:DA_V_DIM + 1, :] for c in range(2)]
    o = num[0] * (1.0 / den[0]) - lam * (num[1] * (1.0 / den[1]))
    o = o * lax.rsqrt(jnp.mean(o * o, axis=0, keepdims=True) + RMS_EPS)
    o = o * (g_ref[...] * (1.0 - LAM_INIT))
    o_ref[0] = o.T.astype(BF16)


def _attention(lam, q, k, vt, bias, g_col, batch, seq):
    bq, bk = ATT_BQ, ATT_BK
    return pl.pallas_call(
        _attn_kernel,
        grid=(batch, DA_HEADS, seq // bq),
        in_specs=[pl.BlockSpec(memory_space=pltpu.SMEM),
                  pl.BlockSpec((1, bq, DA_V_DIM), lambda b, h, i: (b, i, h)),
                  pl.BlockSpec((1, seq, DA_V_DIM), lambda b, h, i: (b, 0, h)),
                  pl.BlockSpec((1, 1, seq // bk, ATT_V_ROWS, bk), lambda b, h, i: (b, h, 0, 0, 0)),
                  pl.BlockSpec((1, ATT_NEAR, bk, bq), lambda b, h, i: (h, 0, 0, 0)),
                  pl.BlockSpec((DA_V_DIM, 1), lambda b, h, i: (0, 0))],
        out_specs=pl.BlockSpec((1, bq, DA_V_DIM), lambda b, h, i: (b, i, h)),
        out_shape=jax.ShapeDtypeStruct((batch, seq, D_MODEL), BF16),
        scratch_shapes=[pltpu.VMEM((2, 1, bq), F32),
                        pltpu.VMEM((2, ATT_V_ROWS, bq), F32),
                        pltpu.VMEM((2, bk, bq), F32),
                        pltpu.VMEM((2, 1, bq), F32),
                        pltpu.VMEM((2, bk, bq), BF16),
                        pltpu.VMEM((2, 1, bq), F32)],
        compiler_params=_params(3),
        name="attention",
    )(lam, q, k, vt, bias, g_col)


def _ssm_kernel(u_ref, w_ref, c_ref, coef_ref, d_ref, y_ref, h_sc):
    t = pl.program_id(2)
    ns = SSM_HALF_STATES

    @pl.when(t == 0)
    def _():
        h_sc[...] = jnp.zeros(h_sc.shape, F32)

    rows = SSM_ROWS
    groups = rows // SUBLANES
    halves = range(SSM_HALVES)

    def local_states(blk):
        u3 = u_ref[0, blk * rows:(blk + 1) * rows, :].reshape(groups, SUBLANES, LANES)
        row = lax.broadcasted_iota(jnp.int32, u3.shape, 1)
        low = lax.broadcasted_iota(jnp.int32, u3.shape, 2) < SSM_HALF_LANES
        delayed = [u3] + [jnp.where(row >= d, pltpu.roll(u3, d, axis=1), 0.0) for d in range(1, SUBLANES)]
        swapped = [pltpu.roll(ud, SSM_HALF_LANES, axis=2) for ud in delayed]
        out = []
        for h in halves:
            pieces = []
            for d in range(0, SUBLANES, 2):
                pair = (jnp.where(low, delayed[d], swapped[d + 1]) if h == 0
                        else jnp.where(low, swapped[d], delayed[d + 1]))
                pieces.append(pair.reshape(rows, LANES).astype(BF16))
            out.append(jnp.dot(jnp.concatenate(pieces, axis=1), w_ref[0, h], preferred_element_type=F32))
        return out

    def carry_and_project(blk, local, state):
        y = None
        new_state = []
        for h in halves:
            pr = coef_ref[0, h, 0]
            pi = coef_ref[0, h, 1]
            hr, hi = state[h]
            xr = local[h][:, :ns].reshape(groups, SUBLANES, ns)
            xi = local[h][:, ns:].reshape(groups, SUBLANES, ns)
            out_r, out_i = [], []
            for g in range(groups):
                gr = xr[g] + (pr * hr - pi * hi)
                gi = xi[g] + (pr * hi + pi * hr)
                hr = gr[SUBLANES - 1:SUBLANES]
                hi = gi[SUBLANES - 1:SUBLANES]
                out_r.append(gr)
                out_i.append(gi)
            new_state.append((hr, hi))
            x_all = jnp.concatenate([jnp.concatenate(out_r, axis=0), jnp.concatenate(out_i, axis=0)], axis=1)
            part = jnp.dot(x_all.astype(BF16), c_ref[0, h], preferred_element_type=F32)
            y = part if y is None else y + part
        sl = slice(blk * rows, (blk + 1) * rows)
        y_ref[0, sl, :] = y + d_ref[...] * u_ref[0, sl, :]
        return new_state

    n_blk = SSM_T // rows
    state = [(h_sc[h, 0], h_sc[h, 1]) for h in halves]
    pending = local_states(0)
    for blk in range(n_blk):
        upcoming = local_states(blk + 1) if blk + 1 < n_blk else None
        state = carry_and_project(blk, pending, state)
        pending = upcoming
    for h in halves:
        h_sc[h, 0] = state[h][0]
        h_sc[h, 1] = state[h][1]


def _ssm(u, w_blk, c_blk, coef, d_skip, batch, seq):
    nb = SSM_LANE_BLOCKS
    act = pl.BlockSpec((1, SSM_T, LANES), lambda b, j, t: (b, t, j))
    return pl.pallas_call(
        _ssm_kernel,
        grid=(batch, nb, seq // SSM_T),
        in_specs=[act,
                  pl.BlockSpec((1, SSM_HALVES, SUBLANES * SSM_HALF_LANES, 2 * SSM_HALF_STATES),
                               lambda b, j, t: (j, 0, 0, 0)),
                  pl.BlockSpec((1, SSM_HALVES, 2 * SSM_HALF_STATES, LANES), lambda b, j, t: (j, 0, 0, 0)),
                  pl.BlockSpec((1, SSM_HALVES, 2, SUBLANES, SSM_HALF_STATES), lambda b, j, t: (j, 0, 0, 0, 0)),
                  pl.BlockSpec((1, LANES), lambda b, j, t: (0, j))],
        out_specs=act,
        out_shape=jax.ShapeDtypeStruct((batch, seq, D_MODEL), F32),
        scratch_shapes=[pltpu.VMEM((SSM_HALVES, 2, 1, SSM_HALF_STATES), F32)],
        compiler_params=_params(3),
        name="ssm",
    )(u, w_blk, c_blk, coef, d_skip)


def _mem_kv_kernel(mem_ref, g_ref, w_ref, k_ref, v_ref):
    mn = _rmsnorm(mem_ref[0], g_ref[...]).astype(BF16)
    kv = jnp.dot(mn, w_ref[...], preferred_element_type=F32)
    k_ref[0] = kv[:, :D_MODEL].astype(BF16)
    v_ref[0] = kv[:, D_MODEL:].astype(BF16)


def _mem_kv(mem, g, w):
    batch, m, _ = mem.shape
    blk = pl.BlockSpec((1, m, D_MODEL), lambda b: (b, 0, 0))
    shape = jax.ShapeDtypeStruct((batch, m, D_MODEL), BF16)
    return pl.pallas_call(
        _mem_kv_kernel,
        grid=(batch,),
        in_specs=[blk, _const_spec((1, D_MODEL)), _const_spec((D_MODEL, 2 * D_MODEL))],
        out_specs=[blk, blk],
        out_shape=[shape, shape],
        compiler_params=_params(1),
        name="mem_kv",
    )(mem, g, w)


def _gelu_tanh(x):
    return 0.5 * x * (1.0 + jnp.tanh(math.sqrt(2.0 / math.pi) * (x + 0.044715 * (x * x * x))))


def _sigmoid(x):
    return 1.0 / (1.0 + jnp.exp(-x))


def _merge_kernel(x_ref, ya_ref, ys_ref, xq_ref, km_ref, vm_ref, g1_ref, wg_ref, glu_w_ref, glu_b_ref,
                  wa_ref, ws_ref, wx_ref, wo_ref, o_ref):
    d = D_MODEL
    x = x_ref[...]
    h = _rmsnorm(x, g1_ref[...]).astype(BF16)

    mixed = _sigmoid(jnp.dot(h, wg_ref[:, 0:d], preferred_element_type=F32)) * jnp.dot(
        ya_ref[...], wa_ref[...], preferred_element_type=F32)

    z = _gelu_tanh(ys_ref[...])
    zb = z.astype(BF16)
    y_ssm = z * _sigmoid(jnp.dot(zb, glu_w_ref[...], preferred_element_type=F32) + glu_b_ref[...])
    mixed = mixed + _sigmoid(jnp.dot(h, wg_ref[:, d:2 * d], preferred_element_type=F32)) * jnp.dot(
        y_ssm.astype(BF16), ws_ref[...], preferred_element_type=F32)

    heads = []
    for hd in range(XA_HEADS):
        sl = slice(hd * XA_HEAD_DIM, (hd + 1) * XA_HEAD_DIM)
        s = lax.dot_general(xq_ref[:, sl], km_ref[0, :, sl], (((1,), (1,)), ((), ())),
                            preferred_element_type=F32) * (XA_HEAD_DIM ** -0.5)
        p = jnp.exp(s - jnp.max(s, axis=-1, keepdims=True))
        p = p * (1.0 / jnp.sum(p, axis=-1, keepdims=True))
        heads.append(jnp.dot(p.astype(BF16), vm_ref[0, :, sl], preferred_element_type=F32))
    y_x = jnp.concatenate(heads, axis=1).astype(BF16)
    mixed = mixed + _sigmoid(jnp.dot(h, wg_ref[:, 2 * d:3 * d], preferred_element_type=F32)) * jnp.dot(
        y_x, wx_ref[...], preferred_element_type=F32)

    o_ref[...] = x + jnp.dot(mixed.astype(BF16), wo_ref[...], preferred_element_type=F32)


def _merge(x2d, y_attn, y_s, xq, kmem, vmem, g1, wg, glu_w, glu_b, wa, ws, wx, wo, seq):
    n = x2d.shape[0]
    tiles_per_batch = seq // MERGE_TM
    m = kmem.shape[1]
    row = pl.BlockSpec((MERGE_TM, D_MODEL), lambda i: (i, 0))
    mem = pl.BlockSpec((1, m, D_MODEL), lambda i: (i // tiles_per_batch, 0, 0))
    sq = _const_spec((D_MODEL, D_MODEL))
    vec = _const_spec((1, D_MODEL))
    return pl.pallas_call(
        _merge_kernel,
        grid=(n // MERGE_TM,),
        in_specs=[row, row, row, row, mem, mem, vec, _const_spec((D_MODEL, 3 * D_MODEL)), sq, vec,
                  sq, sq, sq, sq],
        out_specs=row,
        out_shape=jax.ShapeDtypeStruct((n, D_MODEL), F32),
        compiler_params=_params(1),
        name="merge",
    )(x2d, y_attn, y_s, xq, kmem, vmem, g1, wg, glu_w, glu_b, wa, ws, wx, wo)


def _ffn_kernel(x_ref, g2_ref, wi_ref, wo_ref, gf_ref, o_ref):
    x = x_ref[...]
    h = _rmsnorm(x, g2_ref[...]).astype(BF16)
    acc = x
    for c in range(FFN_HIDDEN // FFN_CHUNK):
        lo = c * FFN_CHUNK
        gate = jnp.dot(h, wi_ref[:, lo:lo + FFN_CHUNK], preferred_element_type=F32)
        up = jnp.dot(h, wi_ref[:, FFN_HIDDEN + lo:FFN_HIDDEN + lo + FFN_CHUNK], preferred_element_type=F32)
        act = (gate * _sigmoid(gate) * up).astype(BF16)
        acc = acc + jnp.dot(act, wo_ref[lo:lo + FFN_CHUNK, :], preferred_element_type=F32)
    o_ref[...] = _rmsnorm(acc, gf_ref[...])


def _ffn(x2d, g2, wi, wo, gf):
    n = x2d.shape[0]
    row = pl.BlockSpec((FFN_TM, D_MODEL), lambda i: (i, 0))
    vec = _const_spec((1, D_MODEL))
    return pl.pallas_call(
        _ffn_kernel,
        grid=(n // FFN_TM,),
        in_specs=[row, vec, _const_spec((D_MODEL, 2 * FFN_HIDDEN)), _const_spec((FFN_HIDDEN, D_MODEL)), vec],
        out_specs=row,
        out_shape=jax.ShapeDtypeStruct((n, D_MODEL), F32),
        compiler_params=_params(1),
        name="ffn",
    )(x2d, g2, wi, wo, gf)


def kernel(x, mem, norm1_g, w_in, da_lq1, da_lk1, da_lq2, da_lk2, da_subln_g, rel_bias, ssm_a_re, ssm_a_im, ssm_log_dt, ssm_b_re, ssm_b_im, ssm_c_re, ssm_c_im, ssm_d, glu_w, glu_b, mem_norm_g, w_mem_kv, w_br_attn, w_br_ssm, w_br_xattn, w_out, norm2_g, w_ffn_in, w_ffn_out, final_g):
    batch, seq, d = x.shape
    depth = w_in.shape[0]
    assert depth == 1 and d == D_MODEL and seq % PROJ_TM == 0
    layer = 0
    x2d = x.reshape(batch * seq, d)
    row = lambda v: v.reshape(1, -1).astype(F32)

    w = w_in[layer]
    scale = DA_HEAD_DIM ** -0.5 * LOG2E
    w4 = jnp.concatenate([w[:, 0:d] * scale, w[:, d:2 * d], w[:, 3 * d:4 * d], w[:, 4 * d:5 * d]],
                         axis=1).astype(BF16)
    wvt = jnp.transpose(w[:, 2 * d:3 * d]).astype(BF16)
    wg = w[:, 5 * d:8 * d].astype(BF16)

    bias, lam_tile = _prep_attn(rel_bias, row(da_lq1[layer]), row(da_lk1[layer]), row(da_lq2[layer]),
                                row(da_lk2[layer]))
    lam = lam_tile[0:1, 0:1]

    pow_re, pow_im, w_re, w_im = _prep_ssm(ssm_a_re[layer], ssm_a_im[layer], ssm_log_dt[layer],
                                           ssm_b_re[layer], ssm_b_im[layer])
    w_blk, c_blk, coef = _ssm_tables(pow_re, pow_im, w_re, w_im, ssm_c_re[layer], ssm_c_im[layer])

    q, k, u, xq, vt = _proj(x2d, row(norm1_g[layer]), w4, wvt, batch, seq)
    y_attn = _attention(lam, q.reshape(batch, seq, d), k.reshape(batch, seq, d), vt, bias,
                        da_subln_g[layer].reshape(DA_V_DIM, 1).astype(F32), batch, seq)
    y_s = _ssm(u.reshape(batch, seq, d), w_blk, c_blk, coef, row(ssm_d[layer]), batch, seq)
    kmem, vmem = _mem_kv(mem, row(mem_norm_g[layer]), w_mem_kv[layer].astype(BF16))
    x_mid = _merge(x2d, y_attn.reshape(batch * seq, d), y_s.reshape(batch * seq, d), xq, kmem, vmem,
                   row(norm1_g[layer]), wg, glu_w[layer].astype(BF16), row(glu_b[layer]),
                   w_br_attn[layer].astype(BF16), w_br_ssm[layer].astype(BF16), w_br_xattn[layer].astype(BF16),
                   w_out[layer].astype(BF16), seq)
    out = _ffn(x_mid, row(norm2_g[layer]), w_ffn_in[layer].astype(BF16), w_ffn_out[layer].astype(BF16),
               row(final_g))
    return out.reshape(batch, seq, d)
```

```python
import math

import jax
import jax.numpy as jnp
import numpy as np
from jax import lax
from jax.experimental import pallas as pl
from jax.experimental.pallas import tpu as pltpu

F32 = jnp.float32
BF16 = jnp.bfloat16

D_MODEL = 1024
CHUNK = 64
DA_HEADS = 8
DA_HEAD_DIM = 64
DA_V_DIM = 128
SSM_GROUP = 16
SSM_GROUPS = 64
SSM_STATE = 64
XA_HEADS = 4
XA_HEAD_DIM = 256
REL_BUCKETS = 32
REL_MAX_DIST = 256
FFN_HIDDEN = 2816
RMS_EPS = 1e-6
LAM_INIT = 0.8 - 0.6 * math.exp(-0.3 * 0)

LANES = 128
SUBLANES = 8
VMEM_LIMIT_BYTES = 56 * 1024 * 1024

ATT_BQ = 512
ATT_BK = 512
ATT_NEAR = ATT_BQ // ATT_BK + 1
ATT_V_ROWS = DA_V_DIM + 16
LOG2E = math.log2(math.e)
PROJ_TM = 512
SSM_T = 512
SSM_ROWS = 128
SSM_LANE_BLOCKS = D_MODEL // LANES
SSM_HALVES = 2
SSM_HALF_LANES = LANES // SSM_HALVES
SSM_GROUPS_PER_HALF = SSM_HALF_LANES // SSM_GROUP
SSM_HALF_STATES = SSM_GROUPS_PER_HALF * SSM_STATE
MERGE_TM = 256
FFN_TM = 512
FFN_CHUNK = 256
MASK_VALUE = -1e30
MASK_BUCKET = REL_BUCKETS


def _params(n_axes, flags=None):
    return pltpu.CompilerParams(dimension_semantics=("arbitrary",) * n_axes,
                                vmem_limit_bytes=VMEM_LIMIT_BYTES, flags=flags)


def _const_spec(shape):
    nd = len(shape)
    return pl.BlockSpec(shape, lambda *_: (0,) * nd, pipeline_mode=pl.Buffered(1))


def _rmsnorm(xf, g):
    return xf * lax.rsqrt(jnp.mean(xf * xf, axis=-1, keepdims=True) + RMS_EPS) * g


def _t5_bucket_np(rel):
    half = REL_BUCKETS // 2
    max_exact = half // 2
    ret = np.where(rel > 0, half, 0)
    n = np.abs(rel)
    nf = np.maximum(n, 1).astype(np.float32)
    large = max_exact + (np.log(nf / np.float32(max_exact)) / np.float32(math.log(REL_MAX_DIST / max_exact))
                         * np.float32(half - max_exact)).astype(np.int32)
    large = np.minimum(large, half - 1)
    return (ret + np.where(n < max_exact, n, large)).astype(np.int32)


def _bucket_tiles():
    kk = np.arange(ATT_BK)[:, None]
    qq = np.arange(ATT_BQ)[None, :]
    tiles = []
    for n in range(ATT_NEAR):
        key = kk + (n - 1) * ATT_BK
        tile = _t5_bucket_np(key - qq)
        tiles.append(np.where((key // CHUNK) <= (qq // CHUNK), tile, MASK_BUCKET))
    nearest_far = (-2 * ATT_BK + ATT_BK - 1) - 0
    assert (_t5_bucket_np(np.arange(-8 * ATT_BQ, nearest_far + 1)) == REL_BUCKETS // 2 - 1).all()
    return np.stack(tiles).astype(np.int32)


def _prep_attn_kernel(relb_ref, bucket_ref, lq1_ref, lk1_ref, lq2_ref, lk2_ref, bias_ref, lam_ref):
    h = pl.program_id(0)
    bk = bucket_ref[...]
    acc = jnp.full(bk.shape, MASK_VALUE, F32)
    for b in range(REL_BUCKETS):
        acc = jnp.where(bk == b, relb_ref[b, h], acc)
    bias_ref[0] = (acc - relb_ref[REL_BUCKETS // 2 - 1, h]) * LOG2E
    lam = (jnp.exp(jnp.sum(lq1_ref[...] * lk1_ref[...], axis=-1, keepdims=True))
           - jnp.exp(jnp.sum(lq2_ref[...] * lk2_ref[...], axis=-1, keepdims=True)) + LAM_INIT)
    lam_ref[...] = jnp.broadcast_to(lam, lam_ref.shape)


def _prep_attn(rel_bias, lq1, lk1, lq2, lk2):
    buckets = jnp.asarray(_bucket_tiles())
    vec = pl.BlockSpec((1, DA_HEAD_DIM), lambda h: (0, 0))
    return pl.pallas_call(
        _prep_attn_kernel,
        grid=(DA_HEADS,),
        in_specs=[pl.BlockSpec(memory_space=pltpu.SMEM),
                  pl.BlockSpec((ATT_NEAR, ATT_BK, ATT_BQ), lambda h: (0, 0, 0)),
                  vec, vec, vec, vec],
        out_specs=[pl.BlockSpec((1, ATT_NEAR, ATT_BK, ATT_BQ), lambda h: (h, 0, 0, 0)),
                   pl.BlockSpec((SUBLANES, LANES), lambda h: (0, 0))],
        out_shape=[jax.ShapeDtypeStruct((DA_HEADS, ATT_NEAR, ATT_BK, ATT_BQ), F32),
                   jax.ShapeDtypeStruct((SUBLANES, LANES), F32)],
        compiler_params=_params(1),
        name="prep_attn",
    )(rel_bias, buckets, lq1, lk1, lq2, lk2)


def _prep_ssm_kernel(are_ref, aim_ref, ldt_ref, bre_ref, bim_ref, pr_ref, pi_ref, wr_ref, wi_ref):
    a_re = are_ref[...]
    a_im = aim_ref[...]
    dt = jnp.exp(ldt_ref[...])
    for n in range(1, SUBLANES + 1):
        mag = jnp.exp(a_re * dt * n)
        ang = a_im * dt * n
        pr_ref[n - 1] = mag * jnp.cos(ang)
        pi_ref[n - 1] = mag * jnp.sin(ang)
    xr = pr_ref[0] - 1.0
    xi = pi_ref[0]
    den = a_re * a_re + a_im * a_im
    cr = ((xr * a_re + xi * a_im) / den)[:, None, :]
    ci = ((xi * a_re - xr * a_im) / den)[:, None, :]
    b_re = bre_ref[...]
    b_im = bim_ref[...]
    bbr = cr * b_re - ci * b_im
    bbi = cr * b_im + ci * b_re
    wr_ref[0] = bbr
    wi_ref[0] = bbi
    for d in range(1, SUBLANES):
        ar = pr_ref[d - 1][:, None, :]
        ai = pi_ref[d - 1][:, None, :]
        wr_ref[d] = ar * bbr - ai * bbi
        wi_ref[d] = ar * bbi + ai * bbr


def _prep_ssm(a_re, a_im, log_dt, b_re, b_im):
    g, p = a_re.shape
    bt_re = jnp.transpose(b_re, (0, 2, 1))
    bt_im = jnp.transpose(b_im, (0, 2, 1))
    pow_shape = jax.ShapeDtypeStruct((SUBLANES, g, p), F32)
    w_shape = jax.ShapeDtypeStruct((SUBLANES, g, SSM_GROUP, p), F32)
    return pl.pallas_call(
        _prep_ssm_kernel,
        out_shape=[pow_shape, pow_shape, w_shape, w_shape],
        name="prep_ssm",
    )(a_re, a_im, log_dt.reshape(g, 1), bt_re, bt_im)


def _ssm_tables(pow_re, pow_im, w_re, w_im, c_re, c_im):
    nb, nh, gh = SSM_LANE_BLOCKS, SSM_HALVES, SSM_GROUPS_PER_HALF
    eye_g = jnp.eye(gh, dtype=bool)
    eye_h = jnp.eye(nh, dtype=bool)

    def w_block(w):
        w = jnp.transpose(w.reshape(SUBLANES, nb, nh, gh, SSM_GROUP, 1, SSM_STATE), (1, 2, 0, 3, 4, 5, 6))
        same_group = eye_g[None, None, None, :, None, :, None]
        return jnp.where(same_group, w, 0.0).reshape(nb, nh, SUBLANES * SSM_HALF_LANES, SSM_HALF_STATES)

    def c_block(cc):
        cc = jnp.transpose(cc.reshape(nb, nh, gh, SSM_GROUP, SSM_STATE), (0, 1, 2, 4, 3))
        cc = cc.reshape(nb, nh, gh, SSM_STATE, 1, 1, SSM_GROUP)
        same = (eye_h[None, :, None, None, :, None, None] & eye_g[None, None, :, None, None, :, None])
        return jnp.where(same, cc, 0.0).reshape(nb, nh, SSM_HALF_STATES, LANES)

    w_blk = jnp.concatenate([w_block(w_re), w_block(w_im)], axis=3).astype(BF16)
    c_blk = jnp.concatenate([c_block(c_re), c_block(-c_im)], axis=2).astype(BF16)

    def lanes(t):
        return jnp.transpose(t.reshape(SUBLANES, nb, nh, SSM_HALF_STATES), (1, 2, 0, 3))

    coef = jnp.stack([lanes(pow_re), lanes(pow_im)], axis=2)
    return w_blk, c_blk, coef


def _proj_kernel(x_ref, g_ref, w_ref, wvt_ref, q_ref, k_ref, u_ref, xq_ref, vt_ref):
    h = _rmsnorm(x_ref[...], g_ref[...]).astype(BF16)
    d = D_MODEL
    q_ref[...] = jnp.dot(h, w_ref[:, 0:d], preferred_element_type=F32).astype(BF16)
    k_ref[...] = jnp.dot(h, w_ref[:, d:2 * d], preferred_element_type=F32).astype(BF16)
    u_ref[...] = jnp.dot(h, w_ref[:, 2 * d:3 * d], preferred_element_type=F32)
    xq_ref[...] = jnp.dot(h, w_ref[:, 3 * d:4 * d], preferred_element_type=F32).astype(BF16)
    vt = lax.dot_general(wvt_ref[...], h, (((1,), (1,)), ((), ())), preferred_element_type=F32).astype(BF16)
    for hd in range(DA_HEADS):
        for jb in range(PROJ_TM // ATT_BK):
            vt_ref[0, hd, jb, 0:DA_V_DIM, :] = vt[hd * DA_V_DIM:(hd + 1) * DA_V_DIM,
                                                   jb * ATT_BK:(jb + 1) * ATT_BK]
            tail_row = lax.broadcasted_iota(jnp.int32, (ATT_V_ROWS - DA_V_DIM, ATT_BK), 0)
            vt_ref[0, hd, jb, DA_V_DIM:ATT_V_ROWS, :] = jnp.where(tail_row == 0, 1.0, 0.0).astype(BF16)


def _proj(x2d, g, w4, wvt, batch, seq):
    n = x2d.shape[0]
    tiles_per_batch = seq // PROJ_TM
    blocks_per_tile = PROJ_TM // ATT_BK
    row = pl.BlockSpec((PROJ_TM, D_MODEL), lambda i: (i, 0))
    act = lambda dt: jax.ShapeDtypeStruct((n, D_MODEL), dt)
    return pl.pallas_call(
        _proj_kernel,
        grid=(n // PROJ_TM,),
        in_specs=[row, _const_spec((1, D_MODEL)), _const_spec((D_MODEL, 4 * D_MODEL)),
                  _const_spec((D_MODEL, D_MODEL))],
        out_specs=[row, row, row, row,
                   pl.BlockSpec((1, DA_HEADS, blocks_per_tile, ATT_V_ROWS, ATT_BK),
                                lambda i: (i // tiles_per_batch, 0, i % tiles_per_batch, 0, 0))],
        out_shape=[act(BF16), act(BF16), act(F32), act(BF16),
                   jax.ShapeDtypeStruct((batch, DA_HEADS, seq // ATT_BK, ATT_V_ROWS, ATT_BK), BF16)],
        compiler_params=_params(1),
        name="proj",
    )(x2d, g, w4, wvt)


def _attn_kernel(lam_ref, q_ref, k_ref, vt_ref, bias_ref, g_ref, o_ref, m_sc, acc_sc, s_sc, bm_sc, p_sc, a_sc):
    ratio = ATT_BQ // ATT_BK
    lam = lam_ref[0, 0]
    gain = g_ref[...] * (1.0 - LAM_INIT)
    lane = lax.broadcasted_iota(jnp.int32, (ATT_BQ, 2 * DA_HEAD_DIM), 1)

    def scores(qs, j, tile):
        kb = k_ref[0, j * ATT_BK:(j + 1) * ATT_BK, :]
        for c in range(2):
            s = lax.dot_general(kb, qs[c], (((1,), (1,)), ((), ())), preferred_element_type=F32)
            if tile is not None:
                s = s + bias_ref[0, tile]
            s_sc[c] = s
            bm_sc[c] = jnp.max(s, axis=0, keepdims=True)

    def values(j):
        vtb = vt_ref[0, 0, j]
        for c in range(2):
            pv = jnp.dot(vtb, p_sc[c], preferred_element_type=F32)
            acc_sc[c] = pv if j == 0 else a_sc[c] * acc_sc[c] + pv

    def softmax(first):
        for c in range(2):
            m_new = bm_sc[c]
            if not first:
                m_old = m_sc[c]
                m_new = jnp.maximum(m_old, m_new)
                a_sc[c] = jnp.exp2(m_old - m_new)
            m_sc[c] = m_new
            p_sc[c] = jnp.exp2(s_sc[c] - m_new).astype(BF16)

    def finish(i):
        num = [acc_sc[c, 0:DA_V_DIM, :] for c in range(2)]
        den = [acc_sc[c, DA_V_DIM:DA_V_DIM + 1, :] for c in range(2)]
        o = num[0] * (1.0 / den[0]) - lam * (num[1] * (1.0 / den[1]))
        o = o * lax.rsqrt(jnp.mean(o * o, axis=0, keepdims=True) + RMS_EPS)
        o_ref[0, i * ATT_BQ:(i + 1) * ATT_BQ, :] = (o * gain).T.astype(BF16)

    for i in range(q_ref.shape[1] // ATT_BQ):
        qh = q_ref[0, i * ATT_BQ:(i + 1) * ATT_BQ, :]
        zero = jnp.zeros_like(qh)
        qs = (jnp.where(lane < DA_HEAD_DIM, qh, zero), jnp.where(lane >= DA_HEAD_DIM, qh, zero))
        first_near = ratio * i - 1
        last = first_near + ATT_NEAR - 1

        def tile_of(j, first_near=first_near):
            return j - first_near if 0 <= j - first_near < ATT_NEAR else None

        scores(qs, 0, tile_of(0))
        for t in range(last + 1):
            if t >= 1:
                values(t - 1)
            softmax(first=(t == 0))
            if t < last:
                scores(qs, t + 1, tile_of(t + 1))
        values(last)
        finish(i)


def _attention(lam, q, k, vt, bias, g_col, batch, seq):
    bq, bk = ATT_BQ, ATT_BK
    whole = pl.BlockSpec((1, seq, DA_V_DIM), lambda b, h: (b, 0, h))
    return pl.pallas_call(
        _attn_kernel,
        grid=(batch, DA_HEADS),
        in_specs=[pl.BlockSpec(memory_space=pltpu.SMEM),
                  whole,
                  whole,
                  pl.BlockSpec((1, 1, seq // bk, ATT_V_ROWS, bk), lambda b, h: (b, h, 0, 0, 0)),
                  pl.BlockSpec((1, ATT_NEAR, bk, bq), lambda b, h: (h, 0, 0, 0)),
                  pl.BlockSpec((DA_V_DIM, 1), lambda b, h: (0, 0))],
        out_specs=whole,
        out_shape=jax.ShapeDtypeStruct((batch, seq, D_MODEL), BF16),
        scratch_shapes=[pltpu.VMEM((2, 1, bq), F32),
                        pltpu.VMEM((2, ATT_V_ROWS, bq), F32),
                        pltpu.VMEM((2, bk, bq), F32),
                        pltpu.VMEM((2, 1, bq), F32),
                        pltpu.VMEM((2, bk, bq), BF16),
                        pltpu.VMEM((2, 1, bq), F32)],
        compiler_params=_params(2),
        name="attention",
    )(lam, q, k, vt, bias, g_col)


def _ssm_kernel(u_ref, w_ref, c_ref, coef_ref, d_ref, y_ref, h_sc):
    t = pl.program_id(2)
    ns = SSM_HALF_STATES

    @pl.when(t == 0)
    def _():
        h_sc[...] = jnp.zeros(h_sc.shape, F32)

    rows = SSM_ROWS
    groups = rows // SUBLANES
    halves = range(SSM_HALVES)

    def local_states(blk):
        u3 = u_ref[0, blk * rows:(blk + 1) * rows, :].reshape(groups, SUBLANES, LANES)
        row = lax.broadcasted_iota(jnp.int32, u3.shape, 1)
        low = lax.broadcasted_iota(jnp.int32, u3.shape, 2) < SSM_HALF_LANES
        delayed = [u3] + [jnp.where(row >= d, pltpu.roll(u3, d, axis=1), 0.0) for d in range(1, SUBLANES)]
        swapped = [pltpu.roll(ud, SSM_HALF_LANES, axis=2) for ud in delayed]
        out = []
        for h in halves:
            pieces = []
            for d in range(0, SUBLANES, 2):
                pair = (jnp.where(low, delayed[d], swapped[d + 1]) if h == 0
                        else jnp.where(low, swapped[d], delayed[d + 1]))
                pieces.append(pair.reshape(rows, LANES).astype(BF16))
            out.append(jnp.dot(jnp.concatenate(pieces, axis=1), w_ref[0, h], preferred_element_type=F32))
        return out

    def carry_and_project(blk, local, state):
        y = None
        new_state = []
        for h in halves:
            pr = coef_ref[0, h, 0]
            pi = coef_ref[0, h, 1]
            hr, hi = state[h]
            xr = local[h][:, :ns].reshape(groups, SUBLANES, ns)
            xi = local[h][:, ns:].reshape(groups, SUBLANES, ns)
            out_r, out_i = [], []
            for g in range(groups):
                gr = xr[g] + (pr * hr - pi * hi)
                gi = xi[g] + (pr * hi + pi * hr)
                hr = gr[SUBLANES - 1:SUBLANES]
                hi = gi[SUBLANES - 1:SUBLANES]
                out_r.append(gr)
                out_i.append(gi)
            new_state.append((hr, hi))
            x_all = jnp.concatenate([jnp.concatenate(out_r, axis=0), jnp.concatenate(out_i, axis=0)], axis=1)
            part = jnp.dot(x_all.astype(BF16), c_ref[0, h], preferred_element_type=F32)
            y = part if y is None else y + part
        sl = slice(blk * rows, (blk + 1) * rows)
        y_ref[0, sl, :] = y + d_ref[...] * u_ref[0, sl, :]
        return new_state

    n_blk = SSM_T // rows
    state = [(h_sc[h, 0], h_sc[h, 1]) for h in halves]
    pending = local_states(0)
    for blk in range(n_blk):
        upcoming = local_states(blk + 1) if blk + 1 < n_blk else None
        state = carry_and_project(blk, pending, state)
        pending = upcoming
    for h in halves:
        h_sc[h, 0] = state[h][0]
        h_sc[h, 1] = state[h][1]


def _ssm(u, w_blk, c_blk, coef, d_skip, batch, seq):
    nb = SSM_LANE_BLOCKS
    act = pl.BlockSpec((1, SSM_T, LANES), lambda b, j, t: (b, t, j))
    return pl.pallas_call(
        _ssm_kernel,
        grid=(batch, nb, seq // SSM_T),
        in_specs=[act,
                  pl.BlockSpec((1, SSM_HALVES, SUBLANES * SSM_HALF_LANES, 2 * SSM_HALF_STATES),
                               lambda b, j, t: (j, 0, 0, 0)),
                  pl.BlockSpec((1, SSM_HALVES, 2 * SSM_HALF_STATES, LANES), lambda b, j, t: (j, 0, 0, 0)),
                  pl.BlockSpec((1, SSM_HALVES, 2, SUBLANES, SSM_HALF_STATES), lambda b, j, t: (j, 0, 0, 0, 0)),
                  pl.BlockSpec((1, LANES), lambda b, j, t: (0, j))],
        out_specs=act,
        out_shape=jax.ShapeDtypeStruct((batch, seq, D_MODEL), F32),
        scratch_shapes=[pltpu.VMEM((SSM_HALVES, 2, 1, SSM_HALF_STATES), F32)],
        compiler_params=_params(3),
        name="ssm",
    )(u, w_blk, c_blk, coef, d_skip)


def _mem_kv_kernel(mem_ref, g_ref, w_ref, k_ref, v_ref):
    mn = _rmsnorm(mem_ref[0], g_ref[...]).astype(BF16)
    kv = jnp.dot(mn, w_ref[...], preferred_element_type=F32)
    k_ref[0] = kv[:, :D_MODEL].astype(BF16)
    v_ref[0] = kv[:, D_MODEL:].astype(BF16)


def _mem_kv(mem, g, w):
    batch, m, _ = mem.shape
    blk = pl.BlockSpec((1, m, D_MODEL), lambda b: (b, 0, 0))
    shape = jax.ShapeDtypeStruct((batch, m, D_MODEL), BF16)
    return pl.pallas_call(
        _mem_kv_kernel,
        grid=(batch,),
        in_specs=[blk, _const_spec((1, D_MODEL)), _const_spec((D_MODEL, 2 * D_MODEL))],
        out_specs=[blk, blk],
        out_shape=[shape, shape],
        compiler_params=_params(1),
        name="mem_kv",
    )(mem, g, w)


def _gelu_tanh(x):
    return 0.5 * x * (1.0 + jnp.tanh(math.sqrt(2.0 / math.pi) * (x + 0.044715 * (x * x * x))))


def _sigmoid(x):
    return 1.0 / (1.0 + jnp.exp(-x))


def _merge_kernel(x_ref, ya_ref, ys_ref, xq_ref, km_ref, vm_ref, g1_ref, wg_ref, glu_w_ref, glu_b_ref,
                  wa_ref, ws_ref, wx_ref, wo_ref, o_ref):
    d = D_MODEL
    x = x_ref[...]
    h = _rmsnorm(x, g1_ref[...]).astype(BF16)

    mixed = _sigmoid(jnp.dot(h, wg_ref[:, 0:d], preferred_element_type=F32)) * jnp.dot(
        ya_ref[...], wa_ref[...], preferred_element_type=F32)

    z = _gelu_tanh(ys_ref[...])
    zb = z.astype(BF16)
    y_ssm = z * _sigmoid(jnp.dot(zb, glu_w_ref[...], preferred_element_type=F32) + glu_b_ref[...])
    mixed = mixed + _sigmoid(jnp.dot(h, wg_ref[:, d:2 * d], preferred_element_type=F32)) * jnp.dot(
        y_ssm.astype(BF16), ws_ref[...], preferred_element_type=F32)

    heads = []
    for hd in range(XA_HEADS):
        sl = slice(hd * XA_HEAD_DIM, (hd + 1) * XA_HEAD_DIM)
        s = lax.dot_general(xq_ref[:, sl], km_ref[0, :, sl], (((1,), (1,)), ((), ())),
                            preferred_element_type=F32) * (XA_HEAD_DIM ** -0.5)
        p = jnp.exp(s - jnp.max(s, axis=-1, keepdims=True))
        p = p * (1.0 / jnp.sum(p, axis=-1, keepdims=True))
        heads.append(jnp.dot(p.astype(BF16), vm_ref[0, :, sl], preferred_element_type=F32))
    y_x = jnp.concatenate(heads, axis=1).astype(BF16)
    mixed = mixed + _sigmoid(jnp.dot(h, wg_ref[:, 2 * d:3 * d], preferred_element_type=F32)) * jnp.dot(
        y_x, wx_ref[...], preferred_element_type=F32)

    o_ref[...] = x + jnp.dot(mixed.astype(BF16), wo_ref[...], preferred_element_type=F32)


def _merge(x2d, y_attn, y_s, xq, kmem, vmem, g1, wg, glu_w, glu_b, wa, ws, wx, wo, seq):
    n = x2d.shape[0]
    tiles_per_batch = seq // MERGE_TM
    m = kmem.shape[1]
    row = pl.BlockSpec((MERGE_TM, D_MODEL), lambda i: (i, 0))
    mem = pl.BlockSpec((1, m, D_MODEL), lambda i: (i // tiles_per_batch, 0, 0))
    sq = _const_spec((D_MODEL, D_MODEL))
    vec = _const_spec((1, D_MODEL))
    return pl.pallas_call(
        _merge_kernel,
        grid=(n // MERGE_TM,),
        in_specs=[row, row, row, row, mem, mem, vec, _const_spec((D_MODEL, 3 * D_MODEL)), sq, vec,
                  sq, sq, sq, sq],
        out_specs=row,
        out_shape=jax.ShapeDtypeStruct((n, D_MODEL), F32),
        compiler_params=_params(1),
        name="merge",
    )(x2d, y_attn, y_s, xq, kmem, vmem, g1, wg, glu_w, glu_b, wa, ws, wx, wo)


def _ffn_kernel(x_ref, g2_ref, wi_ref, wo_ref, gf_ref, o_ref):
    x = x_ref[...]
    h = _rmsnorm(x, g2_ref[...]).astype(BF16)
    acc = x
    for c in range(FFN_HIDDEN // FFN_CHUNK):
        lo = c * FFN_CHUNK
        gate = jnp.dot(h, wi_ref[:, lo:lo + FFN_CHUNK], preferred_element_type=F32)
        up = jnp.dot(h, wi_ref[:, FFN_HIDDEN + lo:FFN_HIDDEN + lo + FFN_CHUNK], preferred_element_type=F32)
        act = (gate * _sigmoid(gate) * up).astype(BF16)
        acc = acc + jnp.dot(act, wo_ref[lo:lo + FFN_CHUNK, :], preferred_element_type=F32)
    o_ref[...] = _rmsnorm(acc, gf_ref[...])


def _ffn(x2d, g2, wi, wo, gf):
    n = x2d.shape[0]
    row = pl.BlockSpec((FFN_TM, D_MODEL), lambda i: (i, 0))
    vec = _const_spec((1, D_MODEL))
    return pl.pallas_call(
        _ffn_kernel,
        grid=(n // FFN_TM,),
        in_specs=[row, vec, _const_spec((D_MODEL, 2 * FFN_HIDDEN)), _const_spec((FFN_HIDDEN, D_MODEL)), vec],
        out_specs=row,
        out_shape=jax.ShapeDtypeStruct((n, D_MODEL), F32),
        compiler_params=_params(1),
        name="ffn",
    )(x2d, g2, wi, wo, gf)


def kernel(x, mem, norm1_g, w_in, da_lq1, da_lk1, da_lq2, da_lk2, da_subln_g, rel_bias, ssm_a_re, ssm_a_im, ssm_log_dt, ssm_b_re, ssm_b_im, ssm_c_re, ssm_c_im, ssm_d, glu_w, glu_b, mem_norm_g, w_mem_kv, w_br_attn, w_br_ssm, w_br_xattn, w_out, norm2_g, w_ffn_in, w_ffn_out, final_g):
    batch, seq, d = x.shape
    depth = w_in.shape[0]
    assert depth == 1 and d == D_MODEL and seq % PROJ_TM == 0
    layer = 0
    x2d = x.reshape(batch * seq, d)
    row = lambda v: v.reshape(1, -1).astype(F32)

    w = w_in[layer]
    scale = DA_HEAD_DIM ** -0.5 * LOG2E
    w4 = jnp.concatenate([w[:, 0:d] * scale, w[:, d:2 * d], w[:, 3 * d:4 * d], w[:, 4 * d:5 * d]],
                         axis=1).astype(BF16)
    wvt = jnp.transpose(w[:, 2 * d:3 * d]).astype(BF16)
    wg = w[:, 5 * d:8 * d].astype(BF16)

    bias, lam_tile = _prep_attn(rel_bias, row(da_lq1[layer]), row(da_lk1[layer]), row(da_lq2[layer]),
                                row(da_lk2[layer]))
    lam = lam_tile[0:1, 0:1]

    pow_re, pow_im, w_re, w_im = _prep_ssm(ssm_a_re[layer], ssm_a_im[layer], ssm_log_dt[layer],
                                           ssm_b_re[layer], ssm_b_im[layer])
    w_blk, c_blk, coef = _ssm_tables(pow_re, pow_im, w_re, w_im, ssm_c_re[layer], ssm_c_im[layer])

    q, k, u, xq, vt = _proj(x2d, row(norm1_g[layer]), w4, wvt, batch, seq)
    y_attn = _attention(lam, q.reshape(batch, seq, d), k.reshape(batch, seq, d), vt, bias,
                        da_subln_g[layer].reshape(DA_V_DIM, 1).astype(F32), batch, seq)
    y_s = _ssm(u.reshape(batch, seq, d), w_blk, c_blk, coef, row(ssm_d[layer]), batch, seq)
    kmem, vmem = _mem_kv(mem, row(mem_norm_g[layer]), w_mem_kv[layer].astype(BF16))
    x_mid = _merge(x2d, y_attn.reshape(batch * seq, d), y_s.reshape(batch * seq, d), xq, kmem, vmem,
                   row(norm1_g[layer]), wg, glu_w[layer].astype(BF16), row(glu_b[layer]),
                   w_br_attn[layer].astype(BF16), w_br_ssm[layer].astype(BF16), w_br_xattn[layer].astype(BF16),
                   w_out[layer].astype(BF16), seq)
    out = _ffn(x_mid, row(norm2_g[layer]), w_ffn_in[layer].astype(BF16), w_ffn_out[layer].astype(BF16),
               row(final_g))
    return out.reshape(batch, seq, d)
```

```python
import functools
import math

import jax
import jax.numpy as jnp
import numpy as np
from jax import lax
from jax.experimental import pallas as pl
from jax.experimental.pallas import tpu as pltpu

F32 = jnp.float32
BF16 = jnp.bfloat16

D_MODEL = 1024
CHUNK = 64
DA_HEADS = 8
DA_HEAD_DIM = 64
DA_V_DIM = 128
SSM_GROUP = 16
SSM_GROUPS = 64
SSM_STATE = 64
XA_HEADS = 4
XA_HEAD_DIM = 256
REL_BUCKETS = 32
REL_MAX_DIST = 256
FFN_HIDDEN = 2816
RMS_EPS = 1e-6
LAM_INIT = 0.8 - 0.6 * math.exp(-0.3 * 0)

LANES = 128
SUBLANES = 8
VMEM_LIMIT_BYTES = 56 * 1024 * 1024

ATT_BQ = 512
ATT_BK = 512
ATT_NEAR = ATT_BQ // ATT_BK + 1
ATT_V_ROWS = DA_V_DIM + 16
LOG2E = math.log2(math.e)
PROJ_TM = 512
SSM_T = 1024
SSM_ROWS = 128
SSM_LANE_BLOCKS = D_MODEL // LANES
SSM_HALVES = 2
SSM_HALF_LANES = LANES // SSM_HALVES
SSM_GROUPS_PER_HALF = SSM_HALF_LANES // SSM_GROUP
SSM_HALF_STATES = SSM_GROUPS_PER_HALF * SSM_STATE
MERGE_TM = 512
FFN_TM = 1024
FFN_CHUNK = 256
MASK_VALUE = -1e30
MASK_BUCKET = REL_BUCKETS


def _params(n_axes, flags=None):
    return pltpu.CompilerParams(dimension_semantics=("arbitrary",) * n_axes,
                                vmem_limit_bytes=VMEM_LIMIT_BYTES, flags=flags)


def _const_spec(shape):
    nd = len(shape)
    return pl.BlockSpec(shape, lambda *_: (0,) * nd, pipeline_mode=pl.Buffered(1))


def _rmsnorm(xf, g):
    return xf * lax.rsqrt(jnp.mean(xf * xf, axis=-1, keepdims=True) + RMS_EPS) * g


def _t5_bucket_np(rel):
    half = REL_BUCKETS // 2
    max_exact = half // 2
    ret = np.where(rel > 0, half, 0)
    n = np.abs(rel)
    nf = np.maximum(n, 1).astype(np.float32)
    large = max_exact + (np.log(nf / np.float32(max_exact)) / np.float32(math.log(REL_MAX_DIST / max_exact))
                         * np.float32(half - max_exact)).astype(np.int32)
    large = np.minimum(large, half - 1)
    return (ret + np.where(n < max_exact, n, large)).astype(np.int32)


def _bucket_tiles():
    kk = np.arange(ATT_BK)[:, None]
    qq = np.arange(ATT_BQ)[None, :]
    tiles = []
    for n in range(ATT_NEAR):
        key = kk + (n - 1) * ATT_BK
        tile = _t5_bucket_np(key - qq)
        tiles.append(np.where((key // CHUNK) <= (qq // CHUNK), tile, MASK_BUCKET))
    nearest_far = (-2 * ATT_BK + ATT_BK - 1) - 0
    assert (_t5_bucket_np(np.arange(-8 * ATT_BQ, nearest_far + 1)) == REL_BUCKETS // 2 - 1).all()
    return np.stack(tiles).astype(np.int32)


def _prep_attn_kernel(present, relb_ref, bucket_ref, lq1_ref, lk1_ref, lq2_ref, lk2_ref, bias_ref, lam_ref):
    h = pl.program_id(0)
    for n, buckets in enumerate(present):
        bk = bucket_ref[n]
        acc = jnp.full(bk.shape, MASK_VALUE, F32)
        for b in buckets:
            acc = jnp.where(bk == b, relb_ref[b, h], acc)
        bias_ref[0, n] = (acc - relb_ref[REL_BUCKETS // 2 - 1, h]) * LOG2E
    lam = (jnp.exp(jnp.sum(lq1_ref[...] * lk1_ref[...], axis=-1, keepdims=True))
           - jnp.exp(jnp.sum(lq2_ref[...] * lk2_ref[...], axis=-1, keepdims=True)) + LAM_INIT)
    lam_ref[...] = jnp.broadcast_to(lam, lam_ref.shape)


def _prep_attn(rel_bias, lq1, lk1, lq2, lk2):
    tiles = _bucket_tiles()
    present = tuple(tuple(int(b) for b in np.unique(t) if b != MASK_BUCKET) for t in tiles)
    buckets = jnp.asarray(tiles)
    vec = pl.BlockSpec((1, DA_HEAD_DIM), lambda h: (0, 0))
    return pl.pallas_call(
        functools.partial(_prep_attn_kernel, present),
        grid=(DA_HEADS,),
        in_specs=[pl.BlockSpec(memory_space=pltpu.SMEM),
                  pl.BlockSpec((ATT_NEAR, ATT_BK, ATT_BQ), lambda h: (0, 0, 0)),
                  vec, vec, vec, vec],
        out_specs=[pl.BlockSpec((1, ATT_NEAR, ATT_BK, ATT_BQ), lambda h: (h, 0, 0, 0)),
                   pl.BlockSpec((SUBLANES, LANES), lambda h: (0, 0))],
        out_shape=[jax.ShapeDtypeStruct((DA_HEADS, ATT_NEAR, ATT_BK, ATT_BQ), F32),
                   jax.ShapeDtypeStruct((SUBLANES, LANES), F32)],
        compiler_params=_params(1),
        name="prep_attn",
    )(rel_bias, buckets, lq1, lk1, lq2, lk2)


def _prep_ssm_kernel(are_ref, aim_ref, ldt_ref, bre_ref, bim_ref, pr_ref, pi_ref, wr_ref, wi_ref):
    a_re = are_ref[...]
    a_im = aim_ref[...]
    dt = jnp.exp(ldt_ref[...])
    for n in range(1, SUBLANES + 1):
        mag = jnp.exp(a_re * dt * n)
        ang = a_im * dt * n
        pr_ref[n - 1] = mag * jnp.cos(ang)
        pi_ref[n - 1] = mag * jnp.sin(ang)
    xr = pr_ref[0] - 1.0
    xi = pi_ref[0]
    den = a_re * a_re + a_im * a_im
    cr = ((xr * a_re + xi * a_im) / den)[:, None, :]
    ci = ((xi * a_re - xr * a_im) / den)[:, None, :]
    b_re = bre_ref[...]
    b_im = bim_ref[...]
    bbr = cr * b_re - ci * b_im
    bbi = cr * b_im + ci * b_re
    wr_ref[0] = bbr
    wi_ref[0] = bbi
    for d in range(1, SUBLANES):
        ar = pr_ref[d - 1][:, None, :]
        ai = pi_ref[d - 1][:, None, :]
        wr_ref[d] = ar * bbr - ai * bbi
        wi_ref[d] = ar * bbi + ai * bbr


def _prep_ssm(a_re, a_im, log_dt, b_re, b_im):
    g, p = a_re.shape
    bt_re = jnp.transpose(b_re, (0, 2, 1))
    bt_im = jnp.transpose(b_im, (0, 2, 1))
    pow_shape = jax.ShapeDtypeStruct((SUBLANES, g, p), F32)
    w_shape = jax.ShapeDtypeStruct((SUBLANES, g, SSM_GROUP, p), F32)
    return pl.pallas_call(
        _prep_ssm_kernel,
        out_shape=[pow_shape, pow_shape, w_shape, w_shape],
        name="prep_ssm",
    )(a_re, a_im, log_dt.reshape(g, 1), bt_re, bt_im)


def _ssm_tables(pow_re, pow_im, w_re, w_im, c_re, c_im):
    nb, nh, gh = SSM_LANE_BLOCKS, SSM_HALVES, SSM_GROUPS_PER_HALF
    eye_g = jnp.eye(gh, dtype=bool)
    eye_h = jnp.eye(nh, dtype=bool)

    def w_block(w):
        w = jnp.transpose(w.reshape(SUBLANES, nb, nh, gh, SSM_GROUP, 1, SSM_STATE), (1, 2, 0, 3, 4, 5, 6))
        same_group = eye_g[None, None, None, :, None, :, None]
        return jnp.where(same_group, w, 0.0).reshape(nb, nh, SUBLANES * SSM_HALF_LANES, SSM_HALF_STATES)

    def c_block(cc):
        cc = jnp.transpose(cc.reshape(nb, nh, gh, SSM_GROUP, SSM_STATE), (0, 1, 2, 4, 3))
        cc = cc.reshape(nb, nh, gh, SSM_STATE, 1, 1, SSM_GROUP)
        same = (eye_h[None, :, None, None, :, None, None] & eye_g[None, None, :, None, None, :, None])
        return jnp.where(same, cc, 0.0).reshape(nb, nh, SSM_HALF_STATES, LANES)

    w_blk = jnp.concatenate([w_block(w_re), w_block(w_im)], axis=3).astype(BF16)
    c_blk = jnp.concatenate([c_block(c_re), c_block(-c_im)], axis=2).astype(BF16)

    def lanes(t):
        return jnp.transpose(t.reshape(SUBLANES, nb, nh, SSM_HALF_STATES), (1, 2, 0, 3))

    coef = jnp.stack([lanes(pow_re), lanes(pow_im)], axis=2)
    return w_blk, c_blk, coef


def _proj_kernel(x_ref, g_ref, w_ref, wvt_ref, q_ref, k_ref, u_ref, xq_ref, vt_ref):
    h = _rmsnorm(x_ref[...], g_ref[...]).astype(BF16)
    d = D_MODEL
    q_ref[...] = jnp.dot(h, w_ref[:, 0:d], preferred_element_type=F32).astype(BF16)
    k_ref[...] = jnp.dot(h, w_ref[:, d:2 * d], preferred_element_type=F32).astype(BF16)
    u_ref[...] = jnp.dot(h, w_ref[:, 2 * d:3 * d], preferred_element_type=F32)
    xq_ref[...] = jnp.dot(h, w_ref[:, 3 * d:4 * d], preferred_element_type=F32).astype(BF16)
    vt = lax.dot_general(wvt_ref[...], h, (((1,), (1,)), ((), ())), preferred_element_type=F32).astype(BF16)
    for hd in range(DA_HEADS):
        for jb in range(PROJ_TM // ATT_BK):
            vt_ref[0, hd, jb, 0:DA_V_DIM, :] = vt[hd * DA_V_DIM:(hd + 1) * DA_V_DIM,
                                                   jb * ATT_BK:(jb + 1) * ATT_BK]
            tail_row = lax.broadcasted_iota(jnp.int32, (ATT_V_ROWS - DA_V_DIM, ATT_BK), 0)
            vt_ref[0, hd, jb, DA_V_DIM:ATT_V_ROWS, :] = jnp.where(tail_row == 0, 1.0, 0.0).astype(BF16)


def _proj(x2d, g, w4, wvt, batch, seq):
    n = x2d.shape[0]
    tiles_per_batch = seq // PROJ_TM
    blocks_per_tile = PROJ_TM // ATT_BK
    row = pl.BlockSpec((PROJ_TM, D_MODEL), lambda i: (i, 0))
    act = lambda dt: jax.ShapeDtypeStruct((n, D_MODEL), dt)
    return pl.pallas_call(
        _proj_kernel,
        grid=(n // PROJ_TM,),
        in_specs=[row, _const_spec((1, D_MODEL)), _const_spec((D_MODEL, 4 * D_MODEL)),
                  _const_spec((D_MODEL, D_MODEL))],
        out_specs=[row, row, row, row,
                   pl.BlockSpec((1, DA_HEADS, blocks_per_tile, ATT_V_ROWS, ATT_BK),
                                lambda i: (i // tiles_per_batch, 0, i % tiles_per_batch, 0, 0))],
        out_shape=[act(BF16), act(BF16), act(F32), act(BF16),
                   jax.ShapeDtypeStruct((batch, DA_HEADS, seq // ATT_BK, ATT_V_ROWS, ATT_BK), BF16)],
        compiler_params=_params(1),
        name="proj",
    )(x2d, g, w4, wvt)


def _attn_kernel(lam_ref, q_ref, k_ref, vt_ref, bias_ref, g_ref, o_ref, m_sc, acc_sc, s_sc, bm_sc, p_sc, a_sc):
    ratio = ATT_BQ // ATT_BK
    lam = lam_ref[0, 0]
    gain = g_ref[...] * (1.0 - LAM_INIT)
    lane = lax.broadcasted_iota(jnp.int32, (ATT_BQ, 2 * DA_HEAD_DIM), 1)

    def scores(qs, j, tile):
        kb = k_ref[0, j * ATT_BK:(j + 1) * ATT_BK, :]
        for c in range(2):
            s = lax.dot_general(kb, qs[c], (((1,), (1,)), ((), ())), preferred_element_type=F32)
            if tile is not None:
                s = s + bias_ref[0, tile]
            s_sc[c] = s
            bm_sc[c] = jnp.max(s, axis=0, keepdims=True)

    def values(j):
        vtb = vt_ref[0, 0, j]
        for c in range(2):
            pv = jnp.dot(vtb, p_sc[c], preferred_element_type=F32)
            acc_sc[c] = pv if j == 0 else a_sc[c] * acc_sc[c] + pv

    def softmax(first):
        for c in range(2):
            m_new = bm_sc[c]
            if not first:
                m_old = m_sc[c]
                m_new = jnp.maximum(m_old, m_new)
                a_sc[c] = jnp.exp2(m_old - m_new)
            m_sc[c] = m_new
            p_sc[c] = jnp.exp2(s_sc[c] - m_new).astype(BF16)

    def finish(i):
        num = [acc_sc[c, 0:DA_V_DIM, :] for c in range(2)]
        den = [acc_sc[c, DA_V_DIM:DA_V_DIM + 1, :] for c in range(2)]
        o = num[0] * (1.0 / den[0]) - lam * (num[1] * (1.0 / den[1]))
        o = o * lax.rsqrt(jnp.mean(o * o, axis=0, keepdims=True) + RMS_EPS)
        o_ref[0, i * ATT_BQ:(i + 1) * ATT_BQ, :] = (o * gain).T.astype(BF16)

    for i in range(q_ref.shape[1] // ATT_BQ):
        qh = q_ref[0, i * ATT_BQ:(i + 1) * ATT_BQ, :]
        zero = jnp.zeros_like(qh)
        qs = (jnp.where(lane < DA_HEAD_DIM, qh, zero), jnp.where(lane >= DA_HEAD_DIM, qh, zero))
        first_near = ratio * i - 1
        last = first_near + ATT_NEAR - 1

        def tile_of(j, first_near=first_near):
            return j - first_near if 0 <= j - first_near < ATT_NEAR else None

        scores(qs, 0, tile_of(0))
        for t in range(last + 1):
            if t >= 1:
                values(t - 1)
            softmax(first=(t == 0))
            if t < last:
                scores(qs, t + 1, tile_of(t + 1))
        values(last)
        finish(i)


def _attention(lam, q, k, vt, bias, g_col, batch, seq):
    bq, bk = ATT_BQ, ATT_BK
    whole = pl.BlockSpec((1, seq, DA_V_DIM), lambda b, h: (b, 0, h))
    return pl.pallas_call(
        _attn_kernel,
        grid=(batch, DA_HEADS),
        in_specs=[pl.BlockSpec(memory_space=pltpu.SMEM),
                  whole,
                  whole,
                  pl.BlockSpec((1, 1, seq // bk, ATT_V_ROWS, bk), lambda b, h: (b, h, 0, 0, 0)),
                  pl.BlockSpec((1, ATT_NEAR, bk, bq), lambda b, h: (h, 0, 0, 0)),
                  pl.BlockSpec((DA_V_DIM, 1), lambda b, h: (0, 0))],
        out_specs=whole,
        out_shape=jax.ShapeDtypeStruct((batch, seq, D_MODEL), BF16),
        scratch_shapes=[pltpu.VMEM((2, 1, bq), F32),
                        pltpu.VMEM((2, ATT_V_ROWS, bq), F32),
                        pltpu.VMEM((2, bk, bq), F32),
                        pltpu.VMEM((2, 1, bq), F32),
                        pltpu.VMEM((2, bk, bq), BF16),
                        pltpu.VMEM((2, 1, bq), F32)],
        compiler_params=_params(2),
        name="attention",
    )(lam, q, k, vt, bias, g_col)


def _ssm_kernel(u_ref, w_ref, c_ref, coef_ref, d_ref, y_ref, h_sc):
    t = pl.program_id(2)
    ns = SSM_HALF_STATES

    @pl.when(t == 0)
    def _():
        h_sc[...] = jnp.zeros(h_sc.shape, F32)

    rows = SSM_ROWS
    groups = rows // SUBLANES
    halves = range(SSM_HALVES)

    def local_states(blk):
        u3 = u_ref[0, blk * rows:(blk + 1) * rows, :].reshape(groups, SUBLANES, LANES)
        row = lax.broadcasted_iota(jnp.int32, u3.shape, 1)
        low = lax.broadcasted_iota(jnp.int32, u3.shape, 2) < SSM_HALF_LANES
        delayed = [u3] + [jnp.where(row >= d, pltpu.roll(u3, d, axis=1), 0.0) for d in range(1, SUBLANES)]
        swapped = [pltpu.roll(ud, SSM_HALF_LANES, axis=2) for ud in delayed]
        out = []
        for h in halves:
            pieces = []
            for d in range(0, SUBLANES, 2):
                pair = (jnp.where(low, delayed[d], swapped[d + 1]) if h == 0
                        else jnp.where(low, swapped[d], delayed[d + 1]))
                pieces.append(pair.reshape(rows, LANES).astype(BF16))
            out.append(jnp.dot(jnp.concatenate(pieces, axis=1), w_ref[0, h], preferred_element_type=F32))
        return out

    def carry_and_project(blk, local, state):
        y = None
        new_state = []
        for h in halves:
            pr = coef_ref[0, h, 0]
            pi = coef_ref[0, h, 1]
            hr, hi = state[h]
            xr = local[h][:, :ns].reshape(groups, SUBLANES, ns)
            xi = local[h][:, ns:].reshape(groups, SUBLANES, ns)
            out_r, out_i = [], []
            for g in range(groups):
                gr = xr[g] + (pr * hr - pi * hi)
                gi = xi[g] + (pr * hi + pi * hr)
                hr = gr[SUBLANES - 1:SUBLANES]
                hi = gi[SUBLANES - 1:SUBLANES]
                out_r.append(gr)
                out_i.append(gi)
            new_state.append((hr, hi))
            x_all = jnp.concatenate([jnp.concatenate(out_r, axis=0), jnp.concatenate(out_i, axis=0)], axis=1)
            part = jnp.dot(x_all.astype(BF16), c_ref[0, h], preferred_element_type=F32)
            y = part if y is None else y + part
        sl = slice(blk * rows, (blk + 1) * rows)
        y_ref[0, sl, :] = y + d_ref[...] * u_ref[0, sl, :]
        return new_state

    n_blk = SSM_T // rows
    state = [(h_sc[h, 0], h_sc[h, 1]) for h in halves]
    pending = local_states(0)
    for blk in range(n_blk):
        upcoming = local_states(blk + 1) if blk + 1 < n_blk else None
        state = carry_and_project(blk, pending, state)
        pending = upcoming
    for h in halves:
        h_sc[h, 0] = state[h][0]
        h_sc[h, 1] = state[h][1]


def _ssm(u, w_blk, c_blk, coef, d_skip, batch, seq):
    nb = SSM_LANE_BLOCKS
    act = pl.BlockSpec((1, SSM_T, LANES), lambda b, j, t: (b, t, j))
    return pl.pallas_call(
        _ssm_kernel,
        grid=(batch, nb, seq // SSM_T),
        in_specs=[act,
                  pl.BlockSpec((1, SSM_HALVES, SUBLANES * SSM_HALF_LANES, 2 * SSM_HALF_STATES),
                               lambda b, j, t: (j, 0, 0, 0)),
                  pl.BlockSpec((1, SSM_HALVES, 2 * SSM_HALF_STATES, LANES), lambda b, j, t: (j, 0, 0, 0)),
                  pl.BlockSpec((1, SSM_HALVES, 2, SUBLANES, SSM_HALF_STATES), lambda b, j, t: (j, 0, 0, 0, 0)),
                  pl.BlockSpec((1, LANES), lambda b, j, t: (0, j))],
        out_specs=act,
        out_shape=jax.ShapeDtypeStruct((batch, seq, D_MODEL), F32),
        scratch_shapes=[pltpu.VMEM((SSM_HALVES, 2, 1, SSM_HALF_STATES), F32)],
        compiler_params=_params(3),
        name="ssm",
    )(u, w_blk, c_blk, coef, d_skip)


def _mem_kv_kernel(mem_ref, g_ref, w_ref, k_ref, v_ref):
    mn = _rmsnorm(mem_ref[0], g_ref[...]).astype(BF16)
    kv = jnp.dot(mn, w_ref[...], preferred_element_type=F32)
    k_ref[0] = kv[:, :D_MODEL].astype(BF16)
    v_ref[0] = kv[:, D_MODEL:].astype(BF16)


def _mem_kv(mem, g, w):
    batch, m, _ = mem.shape
    blk = pl.BlockSpec((1, m, D_MODEL), lambda b: (b, 0, 0))
    shape = jax.ShapeDtypeStruct((batch, m, D_MODEL), BF16)
    return pl.pallas_call(
        _mem_kv_kernel,
        grid=(batch,),
        in_specs=[blk, _const_spec((1, D_MODEL)), _const_spec((D_MODEL, 2 * D_MODEL))],
        out_specs=[blk, blk],
        out_shape=[shape, shape],
        compiler_params=_params(1),
        name="mem_kv",
    )(mem, g, w)


def _gelu_tanh(x):
    return 0.5 * x * (1.0 + jnp.tanh(math.sqrt(2.0 / math.pi) * (x + 0.044715 * (x * x * x))))


def _sigmoid(x):
    return 1.0 / (1.0 + jnp.exp(-x))


def _merge_kernel(x_ref, ya_ref, ys_ref, xq_ref, km_ref, vm_ref, g1_ref, wg_ref, glu_w_ref, glu_b_ref,
                  wa_ref, ws_ref, wx_ref, wo_ref, o_ref):
    d = D_MODEL
    x = x_ref[...]
    h = _rmsnorm(x, g1_ref[...]).astype(BF16)

    mixed = _sigmoid(jnp.dot(h, wg_ref[:, 0:d], preferred_element_type=F32)) * jnp.dot(
        ya_ref[...], wa_ref[...], preferred_element_type=F32)

    z = _gelu_tanh(ys_ref[...])
    zb = z.astype(BF16)
    y_ssm = z * _sigmoid(jnp.dot(zb, glu_w_ref[...], preferred_element_type=F32) + glu_b_ref[...])
    mixed = mixed + _sigmoid(jnp.dot(h, wg_ref[:, d:2 * d], preferred_element_type=F32)) * jnp.dot(
        y_ssm.astype(BF16), ws_ref[...], preferred_element_type=F32)

    heads = []
    for hd in range(XA_HEADS):
        sl = slice(hd * XA_HEAD_DIM, (hd + 1) * XA_HEAD_DIM)
        s = lax.dot_general(xq_ref[:, sl], km_ref[0, :, sl], (((1,), (1,)), ((), ())),
                            preferred_element_type=F32) * (XA_HEAD_DIM ** -0.5)
        p = jnp.exp(s - jnp.max(s, axis=-1, keepdims=True))
        p = p * (1.0 / jnp.sum(p, axis=-1, keepdims=True))
        heads.append(jnp.dot(p.astype(BF16), vm_ref[0, :, sl], preferred_element_type=F32))
    y_x = jnp.concatenate(heads, axis=1).astype(BF16)
    mixed = mixed + _sigmoid(jnp.dot(h, wg_ref[:, 2 * d:3 * d], preferred_element_type=F32)) * jnp.dot(
        y_x, wx_ref[...], preferred_element_type=F32)

    o_ref[...] = x + jnp.dot(mixed.astype(BF16), wo_ref[...], preferred_element_type=F32)


def _merge(x2d, y_attn, y_s, xq, kmem, vmem, g1, wg, glu_w, glu_b, wa, ws, wx, wo, seq):
    n = x2d.shape[0]
    tiles_per_batch = seq // MERGE_TM
    m = kmem.shape[1]
    row = pl.BlockSpec((MERGE_TM, D_MODEL), lambda i: (i, 0))
    mem = pl.BlockSpec((1, m, D_MODEL), lambda i: (i // tiles_per_batch, 0, 0))
    sq = _const_spec((D_MODEL, D_MODEL))
    vec = _const_spec((1, D_MODEL))
    return pl.pallas_call(
        _merge_kernel,
        grid=(n // MERGE_TM,),
        in_specs=[row, row, row, row, mem, mem, vec, _const_spec((D_MODEL, 3 * D_MODEL)), sq, vec,
                  sq, sq, sq, sq],
        out_specs=row,
        out_shape=jax.ShapeDtypeStruct((n, D_MODEL), F32),
        compiler_params=_params(1),
        name="merge",
    )(x2d, y_attn, y_s, xq, kmem, vmem, g1, wg, glu_w, glu_b, wa, ws, wx, wo)


def _ffn_kernel(x_ref, g2_ref, wi_ref, wo_ref, gf_ref, o_ref):
    x = x_ref[...]
    h = _rmsnorm(x, g2_ref[...]).astype(BF16)
    acc = x
    for c in range(FFN_HIDDEN // FFN_CHUNK):
        lo = c * FFN_CHUNK
        gate = jnp.dot(h, wi_ref[:, lo:lo + FFN_CHUNK], preferred_element_type=F32)
        up = jnp.dot(h, wi_ref[:, FFN_HIDDEN + lo:FFN_HIDDEN + lo + FFN_CHUNK], preferred_element_type=F32)
        act = (gate * _sigmoid(gate) * up).astype(BF16)
        acc = acc + jnp.dot(act, wo_ref[lo:lo + FFN_CHUNK, :], preferred_element_type=F32)
    o_ref[...] = _rmsnorm(acc, gf_ref[...])


def _ffn(x2d, g2, wi, wo, gf):
    n = x2d.shape[0]
    row = pl.BlockSpec((FFN_TM, D_MODEL), lambda i: (i, 0))
    vec = _const_spec((1, D_MODEL))
    return pl.pallas_call(
        _ffn_kernel,
        grid=(n // FFN_TM,),
        in_specs=[row, vec, _const_spec((D_MODEL, 2 * FFN_HIDDEN)), _const_spec((FFN_HIDDEN, D_MODEL)), vec],
        out_specs=row,
        out_shape=jax.ShapeDtypeStruct((n, D_MODEL), F32),
        compiler_params=_params(1),
        name="ffn",
    )(x2d, g2, wi, wo, gf)


def kernel(x, mem, norm1_g, w_in, da_lq1, da_lk1, da_lq2, da_lk2, da_subln_g, rel_bias, ssm_a_re, ssm_a_im, ssm_log_dt, ssm_b_re, ssm_b_im, ssm_c_re, ssm_c_im, ssm_d, glu_w, glu_b, mem_norm_g, w_mem_kv, w_br_attn, w_br_ssm, w_br_xattn, w_out, norm2_g, w_ffn_in, w_ffn_out, final_g):
    batch, seq, d = x.shape
    depth = w_in.shape[0]
    assert depth == 1 and d == D_MODEL and seq % PROJ_TM == 0
    layer = 0
    x2d = x.reshape(batch * seq, d)
    row = lambda v: v.reshape(1, -1).astype(F32)

    w = w_in[layer]
    scale = DA_HEAD_DIM ** -0.5 * LOG2E
    w4 = jnp.concatenate([w[:, 0:d] * scale, w[:, d:2 * d], w[:, 3 * d:4 * d], w[:, 4 * d:5 * d]],
                         axis=1).astype(BF16)
    wvt = jnp.transpose(w[:, 2 * d:3 * d]).astype(BF16)
    wg = w[:, 5 * d:8 * d].astype(BF16)

    bias, lam_tile = _prep_attn(rel_bias, row(da_lq1[layer]), row(da_lk1[layer]), row(da_lq2[layer]),
                                row(da_lk2[layer]))
    lam = lam_tile[0:1, 0:1]

    pow_re, pow_im, w_re, w_im = _prep_ssm(ssm_a_re[layer], ssm_a_im[layer], ssm_log_dt[layer],
                                           ssm_b_re[layer], ssm_b_im[layer])
    w_blk, c_blk, coef = _ssm_tables(pow_re, pow_im, w_re, w_im, ssm_c_re[layer], ssm_c_im[layer])

    q, k, u, xq, vt = _proj(x2d, row(norm1_g[layer]), w4, wvt, batch, seq)
    y_attn = _attention(lam, q.reshape(batch, seq, d), k.reshape(batch, seq, d), vt, bias,
                        da_subln_g[layer].reshape(DA_V_DIM, 1).astype(F32), batch, seq)
    y_s = _ssm(u.reshape(batch, seq, d), w_blk, c_blk, coef, row(ssm_d[layer]), batch, seq)
    kmem, vmem = _mem_kv(mem, row(mem_norm_g[layer]), w_mem_kv[layer].astype(BF16))
    x_mid = _merge(x2d, y_attn.reshape(batch * seq, d), y_s.reshape(batch * seq, d), xq, kmem, vmem,
                   row(norm1_g[layer]), wg, glu_w[layer].astype(BF16), row(glu_b[layer]),
                   w_br_attn[layer].astype(BF16), w_br_ssm[layer].astype(BF16), w_br_xattn[layer].astype(BF16),
                   w_out[layer].astype(BF16), seq)
    out = _ffn(x_mid, row(norm2_g[layer]), w_ffn_in[layer].astype(BF16), w_ffn_out[layer].astype(BF16),
               row(final_g))
    return out.reshape(batch, seq, d)
```

```python
import functools
import math

import jax
import jax.numpy as jnp
import numpy as np
from jax import lax
from jax.experimental import pallas as pl
from jax.experimental.pallas import tpu as pltpu

F32 = jnp.float32
BF16 = jnp.bfloat16

D_MODEL = 1024
CHUNK = 64
DA_HEADS = 8
DA_HEAD_DIM = 64
DA_V_DIM = 128
SSM_GROUP = 16
SSM_GROUPS = 64
SSM_STATE = 64
XA_HEADS = 4
XA_HEAD_DIM = 256
REL_BUCKETS = 32
REL_MAX_DIST = 256
FFN_HIDDEN = 2816
RMS_EPS = 1e-6
LAM_INIT = 0.8 - 0.6 * math.exp(-0.3 * 0)

LANES = 128
SUBLANES = 8
VMEM_LIMIT_BYTES = 56 * 1024 * 1024

ATT_BQ = 512
ATT_BK = 512
ATT_NEAR = ATT_BQ // ATT_BK + 1
ATT_V_ROWS = DA_V_DIM + 16
LOG2E = math.log2(math.e)
PROJ_TM = 512
SSM_T = 2048
SSM_ROWS = 256
SSM_LANE_BLOCKS = D_MODEL // LANES
SSM_HALVES = 2
SSM_HALF_LANES = LANES // SSM_HALVES
SSM_GROUPS_PER_HALF = SSM_HALF_LANES // SSM_GROUP
SSM_HALF_STATES = SSM_GROUPS_PER_HALF * SSM_STATE
MERGE_TM = 512
FFN_TM = 1024
FFN_CHUNK = 256
MASK_VALUE = -1e30
MASK_BUCKET = REL_BUCKETS


def _params(n_axes, flags=None):
    return pltpu.CompilerParams(dimension_semantics=("arbitrary",) * n_axes,
                                vmem_limit_bytes=VMEM_LIMIT_BYTES, flags=flags)


def _const_spec(shape):
    nd = len(shape)
    return pl.BlockSpec(shape, lambda *_: (0,) * nd, pipeline_mode=pl.Buffered(1))


def _rmsnorm(xf, g):
    return xf * lax.rsqrt(jnp.mean(xf * xf, axis=-1, keepdims=True) + RMS_EPS) * g


def _t5_bucket_np(rel):
    half = REL_BUCKETS // 2
    max_exact = half // 2
    ret = np.where(rel > 0, half, 0)
    n = np.abs(rel)
    nf = np.maximum(n, 1).astype(np.float32)
    large = max_exact + (np.log(nf / np.float32(max_exact)) / np.float32(math.log(REL_MAX_DIST / max_exact))
                         * np.float32(half - max_exact)).astype(np.int32)
    large = np.minimum(large, half - 1)
    return (ret + np.where(n < max_exact, n, large)).astype(np.int32)


def _bucket_tiles():
    kk = np.arange(ATT_BK)[:, None]
    qq = np.arange(ATT_BQ)[None, :]
    tiles = []
    for n in range(ATT_NEAR):
        key = kk + (n - 1) * ATT_BK
        tile = _t5_bucket_np(key - qq)
        tiles.append(np.where((key // CHUNK) <= (qq // CHUNK), tile, MASK_BUCKET))
    nearest_far = (-2 * ATT_BK + ATT_BK - 1) - 0
    assert (_t5_bucket_np(np.arange(-8 * ATT_BQ, nearest_far + 1)) == REL_BUCKETS // 2 - 1).all()
    return np.stack(tiles).astype(np.int32)


def _prep_attn_kernel(present, relb_ref, bucket_ref, lq1_ref, lk1_ref, lq2_ref, lk2_ref, bias_ref, lam_ref):
    h = pl.program_id(0)
    for n, buckets in enumerate(present):
        bk = bucket_ref[n]
        acc = jnp.full(bk.shape, MASK_VALUE, F32)
        for b in buckets:
            acc = jnp.where(bk == b, relb_ref[b, h], acc)
        bias_ref[0, n] = (acc - relb_ref[REL_BUCKETS // 2 - 1, h]) * LOG2E
    lam = (jnp.exp(jnp.sum(lq1_ref[...] * lk1_ref[...], axis=-1, keepdims=True))
           - jnp.exp(jnp.sum(lq2_ref[...] * lk2_ref[...], axis=-1, keepdims=True)) + LAM_INIT)
    lam_ref[...] = jnp.broadcast_to(lam, lam_ref.shape)


def _prep_attn(rel_bias, lq1, lk1, lq2, lk2):
    tiles = _bucket_tiles()
    present = tuple(tuple(int(b) for b in np.unique(t) if b != MASK_BUCKET) for t in tiles)
    buckets = jnp.asarray(tiles)
    vec = pl.BlockSpec((1, DA_HEAD_DIM), lambda h: (0, 0))
    return pl.pallas_call(
        functools.partial(_prep_attn_kernel, present),
        grid=(DA_HEADS,),
        in_specs=[pl.BlockSpec(memory_space=pltpu.SMEM),
                  pl.BlockSpec((ATT_NEAR, ATT_BK, ATT_BQ), lambda h: (0, 0, 0)),
                  vec, vec, vec, vec],
        out_specs=[pl.BlockSpec((1, ATT_NEAR, ATT_BK, ATT_BQ), lambda h: (h, 0, 0, 0)),
                   pl.BlockSpec((SUBLANES, LANES), lambda h: (0, 0))],
        out_shape=[jax.ShapeDtypeStruct((DA_HEADS, ATT_NEAR, ATT_BK, ATT_BQ), F32),
                   jax.ShapeDtypeStruct((SUBLANES, LANES), F32)],
        compiler_params=_params(1),
        name="prep_attn",
    )(rel_bias, buckets, lq1, lk1, lq2, lk2)


def _prep_ssm_kernel(are_ref, aim_ref, ldt_ref, bre_ref, bim_ref, pr_ref, pi_ref, wr_ref, wi_ref):
    a_re = are_ref[...]
    a_im = aim_ref[...]
    dt = jnp.exp(ldt_ref[...])
    for n in range(1, SUBLANES + 1):
        mag = jnp.exp(a_re * dt * n)
        ang = a_im * dt * n
        pr_ref[n - 1] = mag * jnp.cos(ang)
        pi_ref[n - 1] = mag * jnp.sin(ang)
    xr = pr_ref[0] - 1.0
    xi = pi_ref[0]
    den = a_re * a_re + a_im * a_im
    cr = ((xr * a_re + xi * a_im) / den)[:, None, :]
    ci = ((xi * a_re - xr * a_im) / den)[:, None, :]
    b_re = bre_ref[...]
    b_im = bim_ref[...]
    bbr = cr * b_re - ci * b_im
    bbi = cr * b_im + ci * b_re
    wr_ref[0] = bbr
    wi_ref[0] = bbi
    for d in range(1, SUBLANES):
        ar = pr_ref[d - 1][:, None, :]
        ai = pi_ref[d - 1][:, None, :]
        wr_ref[d] = ar * bbr - ai * bbi
        wi_ref[d] = ar * bbi + ai * bbr


def _prep_ssm(a_re, a_im, log_dt, b_re, b_im):
    g, p = a_re.shape
    bt_re = jnp.transpose(b_re, (0, 2, 1))
    bt_im = jnp.transpose(b_im, (0, 2, 1))
    pow_shape = jax.ShapeDtypeStruct((SUBLANES, g, p), F32)
    w_shape = jax.ShapeDtypeStruct((SUBLANES, g, SSM_GROUP, p), F32)
    return pl.pallas_call(
        _prep_ssm_kernel,
        out_shape=[pow_shape, pow_shape, w_shape, w_shape],
        name="prep_ssm",
    )(a_re, a_im, log_dt.reshape(g, 1), bt_re, bt_im)


def _ssm_tables(pow_re, pow_im, w_re, w_im, c_re, c_im):
    nb, nh, gh = SSM_LANE_BLOCKS, SSM_HALVES, SSM_GROUPS_PER_HALF
    eye_g = jnp.eye(gh, dtype=bool)
    eye_h = jnp.eye(nh, dtype=bool)

    def w_block(w):
        w = jnp.transpose(w.reshape(SUBLANES, nb, nh, gh, SSM_GROUP, 1, SSM_STATE), (1, 2, 0, 3, 4, 5, 6))
        same_group = eye_g[None, None, None, :, None, :, None]
        return jnp.where(same_group, w, 0.0).reshape(nb, nh, SUBLANES * SSM_HALF_LANES, SSM_HALF_STATES)

    def c_block(cc):
        cc = jnp.transpose(cc.reshape(nb, nh, gh, SSM_GROUP, SSM_STATE), (0, 1, 2, 4, 3))
        cc = cc.reshape(nb, nh, gh, SSM_STATE, 1, 1, SSM_GROUP)
        same = (eye_h[None, :, None, None, :, None, None] & eye_g[None, None, :, None, None, :, None])
        return jnp.where(same, cc, 0.0).reshape(nb, nh, SSM_HALF_STATES, LANES)

    w_blk = jnp.concatenate([w_block(w_re), w_block(w_im)], axis=3).astype(BF16)
    c_blk = jnp.concatenate([c_block(c_re), c_block(-c_im)], axis=2).astype(BF16)

    def lanes(t):
        return jnp.transpose(t.reshape(SUBLANES, nb, nh, SSM_HALF_STATES), (1, 2, 0, 3))

    coef = jnp.stack([lanes(pow_re), lanes(pow_im)], axis=2)
    return w_blk, c_blk, coef


def _proj_kernel(x_ref, g_ref, w_ref, wvt_ref, q_ref, k_ref, u_ref, xq_ref, vt_ref):
    h = _rmsnorm(x_ref[...], g_ref[...]).astype(BF16)
    d = D_MODEL
    q_ref[...] = jnp.dot(h, w_ref[:, 0:d], preferred_element_type=F32).astype(BF16)
    k_ref[...] = jnp.dot(h, w_ref[:, d:2 * d], preferred_element_type=F32).astype(BF16)
    u_ref[...] = jnp.dot(h, w_ref[:, 2 * d:3 * d], preferred_element_type=F32)
    xq_ref[...] = jnp.dot(h, w_ref[:, 3 * d:4 * d], preferred_element_type=F32).astype(BF16)
    vt = lax.dot_general(wvt_ref[...], h, (((1,), (1,)), ((), ())), preferred_element_type=F32).astype(BF16)
    for hd in range(DA_HEADS):
        for jb in range(PROJ_TM // ATT_BK):
            vt_ref[0, hd, jb, 0:DA_V_DIM, :] = vt[hd * DA_V_DIM:(hd + 1) * DA_V_DIM,
                                                   jb * ATT_BK:(jb + 1) * ATT_BK]
            tail_row = lax.broadcasted_iota(jnp.int32, (ATT_V_ROWS - DA_V_DIM, ATT_BK), 0)
            vt_ref[0, hd, jb, DA_V_DIM:ATT_V_ROWS, :] = jnp.where(tail_row == 0, 1.0, 0.0).astype(BF16)


def _proj(x2d, g, w4, wvt, batch, seq):
    n = x2d.shape[0]
    tiles_per_batch = seq // PROJ_TM
    blocks_per_tile = PROJ_TM // ATT_BK
    row = pl.BlockSpec((PROJ_TM, D_MODEL), lambda i: (i, 0))
    act = lambda dt: jax.ShapeDtypeStruct((n, D_MODEL), dt)
    return pl.pallas_call(
        _proj_kernel,
        grid=(n // PROJ_TM,),
        in_specs=[row, _const_spec((1, D_MODEL)), _const_spec((D_MODEL, 4 * D_MODEL)),
                  _const_spec((D_MODEL, D_MODEL))],
        out_specs=[row, row, row, row,
                   pl.BlockSpec((1, DA_HEADS, blocks_per_tile, ATT_V_ROWS, ATT_BK),
                                lambda i: (i // tiles_per_batch, 0, i % tiles_per_batch, 0, 0))],
        out_shape=[act(BF16), act(BF16), act(F32), act(BF16),
                   jax.ShapeDtypeStruct((batch, DA_HEADS, seq // ATT_BK, ATT_V_ROWS, ATT_BK), BF16)],
        compiler_params=_params(1),
        name="proj",
    )(x2d, g, w4, wvt)


def _attn_kernel(lam_ref, q_ref, k_ref, vt_ref, bias_ref, g_ref, o_ref, m_sc, acc_sc, s_sc, bm_sc, p_sc, a_sc):
    ratio = ATT_BQ // ATT_BK
    lam = lam_ref[0, 0]
    gain = g_ref[...] * (1.0 - LAM_INIT)
    lane = lax.broadcasted_iota(jnp.int32, (ATT_BQ, 2 * DA_HEAD_DIM), 1)

    def scores(qs, j, tile):
        kb = k_ref[0, j * ATT_BK:(j + 1) * ATT_BK, :]
        for c in range(2):
            s = lax.dot_general(kb, qs[c], (((1,), (1,)), ((), ())), preferred_element_type=F32)
            if tile is not None:
                s = s + bias_ref[0, tile]
            s_sc[c] = s
            bm_sc[c] = jnp.max(s, axis=0, keepdims=True)

    def values(j):
        vtb = vt_ref[0, 0, j]
        for c in range(2):
            pv = jnp.dot(vtb, p_sc[c], preferred_element_type=F32)
            acc_sc[c] = pv if j == 0 else a_sc[c] * acc_sc[c] + pv

    def softmax(first):
        for c in range(2):
            m_new = bm_sc[c]
            if not first:
                m_old = m_sc[c]
                m_new = jnp.maximum(m_old, m_new)
                a_sc[c] = jnp.exp2(m_old - m_new)
            m_sc[c] = m_new
            p_sc[c] = jnp.exp2(s_sc[c] - m_new).astype(BF16)

    def finish(i):
        num = [acc_sc[c, 0:DA_V_DIM, :] for c in range(2)]
        den = [acc_sc[c, DA_V_DIM:DA_V_DIM + 1, :] for c in range(2)]
        o = num[0] * (1.0 / den[0]) - lam * (num[1] * (1.0 / den[1]))
        o = o * lax.rsqrt(jnp.mean(o * o, axis=0, keepdims=True) + RMS_EPS)
        o_ref[0, i * ATT_BQ:(i + 1) * ATT_BQ, :] = (o * gain).T.astype(BF16)

    for i in range(q_ref.shape[1] // ATT_BQ):
        qh = q_ref[0, i * ATT_BQ:(i + 1) * ATT_BQ, :]
        zero = jnp.zeros_like(qh)
        qs = (jnp.where(lane < DA_HEAD_DIM, qh, zero), jnp.where(lane >= DA_HEAD_DIM, qh, zero))
        first_near = ratio * i - 1
        last = first_near + ATT_NEAR - 1

        def tile_of(j, first_near=first_near):
            return j - first_near if 0 <= j - first_near < ATT_NEAR else None

        scores(qs, 0, tile_of(0))
        for t in range(last + 1):
            if t >= 1:
                values(t - 1)
            softmax(first=(t == 0))
            if t < last:
                scores(qs, t + 1, tile_of(t + 1))
        values(last)
        finish(i)


def _attention(lam, q, k, vt, bias, g_col, batch, seq):
    bq, bk = ATT_BQ, ATT_BK
    whole = pl.BlockSpec((1, seq, DA_V_DIM), lambda b, h: (b, 0, h))
    return pl.pallas_call(
        _attn_kernel,
        grid=(batch, DA_HEADS),
        in_specs=[pl.BlockSpec(memory_space=pltpu.SMEM),
                  whole,
                  whole,
                  pl.BlockSpec((1, 1, seq // bk, ATT_V_ROWS, bk), lambda b, h: (b, h, 0, 0, 0)),
                  pl.BlockSpec((1, ATT_NEAR, bk, bq), lambda b, h: (h, 0, 0, 0)),
                  pl.BlockSpec((DA_V_DIM, 1), lambda b, h: (0, 0))],
        out_specs=whole,
        out_shape=jax.ShapeDtypeStruct((batch, seq, D_MODEL), BF16),
        scratch_shapes=[pltpu.VMEM((2, 1, bq), F32),
                        pltpu.VMEM((2, ATT_V_ROWS, bq), F32),
                        pltpu.VMEM((2, bk, bq), F32),
                        pltpu.VMEM((2, 1, bq), F32),
                        pltpu.VMEM((2, bk, bq), BF16),
                        pltpu.VMEM((2, 1, bq), F32)],
        compiler_params=_params(2),
        name="attention",
    )(lam, q, k, vt, bias, g_col)


def _ssm_kernel(u_ref, w_ref, c_ref, coef_ref, d_ref, y_ref, h_sc):
    t = pl.program_id(2)
    ns = SSM_HALF_STATES

    @pl.when(t == 0)
    def _():
        h_sc[...] = jnp.zeros(h_sc.shape, F32)

    rows = SSM_ROWS
    groups = rows // SUBLANES
    halves = range(SSM_HALVES)

    def local_states(blk):
        u3 = u_ref[0, blk * rows:(blk + 1) * rows, :].reshape(groups, SUBLANES, LANES)
        row = lax.broadcasted_iota(jnp.int32, u3.shape, 1)
        low = lax.broadcasted_iota(jnp.int32, u3.shape, 2) < SSM_HALF_LANES
        delayed = [u3] + [jnp.where(row >= d, pltpu.roll(u3, d, axis=1), 0.0) for d in range(1, SUBLANES)]
        swapped = [pltpu.roll(ud, SSM_HALF_LANES, axis=2) for ud in delayed]
        out = []
        for h in halves:
            pieces = []
            for d in range(0, SUBLANES, 2):
                pair = (jnp.where(low, delayed[d], swapped[d + 1]) if h == 0
                        else jnp.where(low, swapped[d], delayed[d + 1]))
                pieces.append(pair.reshape(rows, LANES).astype(BF16))
            out.append(jnp.dot(jnp.concatenate(pieces, axis=1), w_ref[0, h], preferred_element_type=F32))
        return out

    def carry_and_project(blk, local, state):
        y = None
        new_state = []
        for h in halves:
            pr = coef_ref[0, h, 0]
            pi = coef_ref[0, h, 1]
            hr, hi = state[h]
            xr = local[h][:, :ns].reshape(groups, SUBLANES, ns)
            xi = local[h][:, ns:].reshape(groups, SUBLANES, ns)
            out_r, out_i = [], []
            for g in range(groups):
                gr = xr[g] + (pr * hr - pi * hi)
                gi = xi[g] + (pr * hi + pi * hr)
                hr = gr[SUBLANES - 1:SUBLANES]
                hi = gi[SUBLANES - 1:SUBLANES]
                out_r.append(gr)
                out_i.append(gi)
            new_state.append((hr, hi))
            x_all = jnp.concatenate([jnp.concatenate(out_r, axis=0), jnp.concatenate(out_i, axis=0)], axis=1)
            part = jnp.dot(x_all.astype(BF16), c_ref[0, h], preferred_element_type=F32)
            y = part if y is None else y + part
        sl = slice(blk * rows, (blk + 1) * rows)
        y_ref[0, sl, :] = y + d_ref[...] * u_ref[0, sl, :]
        return new_state

    n_blk = SSM_T // rows
    state = [(h_sc[h, 0], h_sc[h, 1]) for h in halves]
    pending = local_states(0)
    for blk in range(n_blk):
        upcoming = local_states(blk + 1) if blk + 1 < n_blk else None
        state = carry_and_project(blk, pending, state)
        pending = upcoming
    for h in halves:
        h_sc[h, 0] = state[h][0]
        h_sc[h, 1] = state[h][1]


def _ssm(u, w_blk, c_blk, coef, d_skip, batch, seq):
    nb = SSM_LANE_BLOCKS
    act = pl.BlockSpec((1, SSM_T, LANES), lambda b, j, t: (b, t, j))
    return pl.pallas_call(
        _ssm_kernel,
        grid=(batch, nb, seq // SSM_T),
        in_specs=[act,
                  pl.BlockSpec((1, SSM_HALVES, SUBLANES * SSM_HALF_LANES, 2 * SSM_HALF_STATES),
                               lambda b, j, t: (j, 0, 0, 0)),
                  pl.BlockSpec((1, SSM_HALVES, 2 * SSM_HALF_STATES, LANES), lambda b, j, t: (j, 0, 0, 0)),
                  pl.BlockSpec((1, SSM_HALVES, 2, SUBLANES, SSM_HALF_STATES), lambda b, j, t: (j, 0, 0, 0, 0)),
                  pl.BlockSpec((1, LANES), lambda b, j, t: (0, j))],
        out_specs=act,
        out_shape=jax.ShapeDtypeStruct((batch, seq, D_MODEL), F32),
        scratch_shapes=[pltpu.VMEM((SSM_HALVES, 2, 1, SSM_HALF_STATES), F32)],
        compiler_params=_params(3),
        name="ssm",
    )(u, w_blk, c_blk, coef, d_skip)


def _mem_kv_kernel(mem_ref, g_ref, w_ref, k_ref, v_ref):
    mn = _rmsnorm(mem_ref[0], g_ref[...]).astype(BF16)
    kv = jnp.dot(mn, w_ref[...], preferred_element_type=F32)
    k_ref[0] = kv[:, :D_MODEL].astype(BF16)
    v_ref[0] = kv[:, D_MODEL:].astype(BF16)


def _mem_kv(mem, g, w):
    batch, m, _ = mem.shape
    blk = pl.BlockSpec((1, m, D_MODEL), lambda b: (b, 0, 0))
    shape = jax.ShapeDtypeStruct((batch, m, D_MODEL), BF16)
    return pl.pallas_call(
        _mem_kv_kernel,
        grid=(batch,),
        in_specs=[blk, _const_spec((1, D_MODEL)), _const_spec((D_MODEL, 2 * D_MODEL))],
        out_specs=[blk, blk],
        out_shape=[shape, shape],
        compiler_params=_params(1),
        name="mem_kv",
    )(mem, g, w)


def _gelu_tanh(x):
    return 0.5 * x * (1.0 + jnp.tanh(math.sqrt(2.0 / math.pi) * (x + 0.044715 * (x * x * x))))


def _sigmoid(x):
    return 1.0 / (1.0 + jnp.exp(-x))


def _merge_kernel(x_ref, ya_ref, ys_ref, xq_ref, km_ref, vm_ref, g1_ref, wg_ref, glu_w_ref, glu_b_ref,
                  wa_ref, ws_ref, wx_ref, wo_ref, o_ref):
    d = D_MODEL
    x = x_ref[...]
    h = _rmsnorm(x, g1_ref[...]).astype(BF16)

    mixed = _sigmoid(jnp.dot(h, wg_ref[:, 0:d], preferred_element_type=F32)) * jnp.dot(
        ya_ref[...], wa_ref[...], preferred_element_type=F32)

    z = _gelu_tanh(ys_ref[...])
    zb = z.astype(BF16)
    y_ssm = z * _sigmoid(jnp.dot(zb, glu_w_ref[...], preferred_element_type=F32) + glu_b_ref[...])
    mixed = mixed + _sigmoid(jnp.dot(h, wg_ref[:, d:2 * d], preferred_element_type=F32)) * jnp.dot(
        y_ssm.astype(BF16), ws_ref[...], preferred_element_type=F32)

    heads = []
    for hd in range(XA_HEADS):
        sl = slice(hd * XA_HEAD_DIM, (hd + 1) * XA_HEAD_DIM)
        s = lax.dot_general(xq_ref[:, sl], km_ref[0, :, sl], (((1,), (1,)), ((), ())),
                            preferred_element_type=F32) * (XA_HEAD_DIM ** -0.5)
        p = jnp.exp(s - jnp.max(s, axis=-1, keepdims=True))
        p = p * (1.0 / jnp.sum(p, axis=-1, keepdims=True))
        heads.append(jnp.dot(p.astype(BF16), vm_ref[0, :, sl], preferred_element_type=F32))
    y_x = jnp.concatenate(heads, axis=1).astype(BF16)
    mixed = mixed + _sigmoid(jnp.dot(h, wg_ref[:, 2 * d:3 * d], preferred_element_type=F32)) * jnp.dot(
        y_x, wx_ref[...], preferred_element_type=F32)

    o_ref[...] = x + jnp.dot(mixed.astype(BF16), wo_ref[...], preferred_element_type=F32)


def _merge(x2d, y_attn, y_s, xq, kmem, vmem, g1, wg, glu_w, glu_b, wa, ws, wx, wo, seq):
    n = x2d.shape[0]
    tiles_per_batch = seq // MERGE_TM
    m = kmem.shape[1]
    row = pl.BlockSpec((MERGE_TM, D_MODEL), lambda i: (i, 0))
    mem = pl.BlockSpec((1, m, D_MODEL), lambda i: (i // tiles_per_batch, 0, 0))
    sq = _const_spec((D_MODEL, D_MODEL))
    vec = _const_spec((1, D_MODEL))
    return pl.pallas_call(
        _merge_kernel,
        grid=(n // MERGE_TM,),
        in_specs=[row, row, row, row, mem, mem, vec, _const_spec((D_MODEL, 3 * D_MODEL)), sq, vec,
                  sq, sq, sq, sq],
        out_specs=row,
        out_shape=jax.ShapeDtypeStruct((n, D_MODEL), F32),
        compiler_params=_params(1),
        name="merge",
    )(x2d, y_attn, y_s, xq, kmem, vmem, g1, wg, glu_w, glu_b, wa, ws, wx, wo)


def _ffn_kernel(x_ref, g2_ref, wi_ref, wo_ref, gf_ref, o_ref):
    x = x_ref[...]
    h = _rmsnorm(x, g2_ref[...]).astype(BF16)
    acc = x
    for c in range(FFN_HIDDEN // FFN_CHUNK):
        lo = c * FFN_CHUNK
        gate = jnp.dot(h, wi_ref[:, lo:lo + FFN_CHUNK], preferred_element_type=F32)
        up = jnp.dot(h, wi_ref[:, FFN_HIDDEN + lo:FFN_HIDDEN + lo + FFN_CHUNK], preferred_element_type=F32)
        act = (gate * _sigmoid(gate) * up).astype(BF16)
        acc = acc + jnp.dot(act, wo_ref[lo:lo + FFN_CHUNK, :], preferred_element_type=F32)
    o_ref[...] = _rmsnorm(acc, gf_ref[...])


def _ffn(x2d, g2, wi, wo, gf):
    n = x2d.shape[0]
    row = pl.BlockSpec((FFN_TM, D_MODEL), lambda i: (i, 0))
    vec = _const_spec((1, D_MODEL))
    return pl.pallas_call(
        _ffn_kernel,
        grid=(n // FFN_TM,),
        in_specs=[row, vec, _const_spec((D_MODEL, 2 * FFN_HIDDEN)), _const_spec((FFN_HIDDEN, D_MODEL)), vec],
        out_specs=row,
        out_shape=jax.ShapeDtypeStruct((n, D_MODEL), F32),
        compiler_params=_params(1),
        name="ffn",
    )(x2d, g2, wi, wo, gf)


def kernel(x, mem, norm1_g, w_in, da_lq1, da_lk1, da_lq2, da_lk2, da_subln_g, rel_bias, ssm_a_re, ssm_a_im, ssm_log_dt, ssm_b_re, ssm_b_im, ssm_c_re, ssm_c_im, ssm_d, glu_w, glu_b, mem_norm_g, w_mem_kv, w_br_attn, w_br_ssm, w_br_xattn, w_out, norm2_g, w_ffn_in, w_ffn_out, final_g):
    batch, seq, d = x.shape
    depth = w_in.shape[0]
    assert depth == 1 and d == D_MODEL and seq % PROJ_TM == 0
    layer = 0
    x2d = x.reshape(batch * seq, d)
    row = lambda v: v.reshape(1, -1).astype(F32)

    w = w_in[layer]
    scale = DA_HEAD_DIM ** -0.5 * LOG2E
    w4 = jnp.concatenate([w[:, 0:d] * scale, w[:, d:2 * d], w[:, 3 * d:4 * d], w[:, 4 * d:5 * d]],
                         axis=1).astype(BF16)
    wvt = jnp.transpose(w[:, 2 * d:3 * d]).astype(BF16)
    wg = w[:, 5 * d:8 * d].astype(BF16)

    bias, lam_tile = _prep_attn(rel_bias, row(da_lq1[layer]), row(da_lk1[layer]), row(da_lq2[layer]),
                                row(da_lk2[layer]))
    lam = lam_tile[0:1, 0:1]

    pow_re, pow_im, w_re, w_im = _prep_ssm(ssm_a_re[layer], ssm_a_im[layer], ssm_log_dt[layer],
                                           ssm_b_re[layer], ssm_b_im[layer])
    w_blk, c_blk, coef = _ssm_tables(pow_re, pow_im, w_re, w_im, ssm_c_re[layer], ssm_c_im[layer])

    q, k, u, xq, vt = _proj(x2d, row(norm1_g[layer]), w4, wvt, batch, seq)
    y_attn = _attention(lam, q.reshape(batch, seq, d), k.reshape(batch, seq, d), vt, bias,
                        da_subln_g[layer].reshape(DA_V_DIM, 1).astype(F32), batch, seq)
    y_s = _ssm(u.reshape(batch, seq, d), w_blk, c_blk, coef, row(ssm_d[layer]), batch, seq)
    kmem, vmem = _mem_kv(mem, row(mem_norm_g[layer]), w_mem_kv[layer].astype(BF16))
    x_mid = _merge(x2d, y_attn.reshape(batch * seq, d), y_s.reshape(batch * seq, d), xq, kmem, vmem,
                   row(norm1_g[layer]), wg, glu_w[layer].astype(BF16), row(glu_b[layer]),
                   w_br_attn[layer].astype(BF16), w_br_ssm[layer].astype(BF16), w_br_xattn[layer].astype(BF16),
                   w_out[layer].astype(BF16), seq)
    out = _ffn(x_mid, row(norm2_g[layer]), w_ffn_in[layer].astype(BF16), w_ffn_out[layer].astype(BF16),
               row(final_g))
    return out.reshape(batch, seq, d)
```

```python
import functools
import math

import jax
import jax.numpy as jnp
import numpy as np
from jax import lax
from jax.experimental import pallas as pl
from jax.experimental.pallas import tpu as pltpu

F32 = jnp.float32
BF16 = jnp.bfloat16

D_MODEL = 1024
CHUNK = 64
DA_HEADS = 8
DA_HEAD_DIM = 64
DA_V_DIM = 128
SSM_GROUP = 16
SSM_GROUPS = 64
SSM_STATE = 64
XA_HEADS = 4
XA_HEAD_DIM = 256
REL_BUCKETS = 32
REL_MAX_DIST = 256
FFN_HIDDEN = 2816
RMS_EPS = 1e-6
LAM_INIT = 0.8 - 0.6 * math.exp(-0.3 * 0)

LANES = 128
SUBLANES = 8
VMEM_LIMIT_BYTES = 56 * 1024 * 1024

ATT_BQ = 512
ATT_BK = 512
ATT_NEAR = ATT_BQ // ATT_BK + 1
ATT_V_ROWS = DA_V_DIM + 16
LOG2E = math.log2(math.e)
PROJ_TM = 512
SSM_T = 2048
SSM_ROWS = 256
SSM_LANE_BLOCKS = D_MODEL // LANES
SSM_HALVES = 2
SSM_HALF_LANES = LANES // SSM_HALVES
SSM_GROUPS_PER_HALF = SSM_HALF_LANES // SSM_GROUP
SSM_HALF_STATES = SSM_GROUPS_PER_HALF * SSM_STATE
MERGE_TM = 512
FFN_TM = 1024
FFN_CHUNK = 256
MASK_VALUE = -1e30
MASK_BUCKET = REL_BUCKETS


def _params(n_axes, flags=None):
    return pltpu.CompilerParams(dimension_semantics=("arbitrary",) * n_axes,
                                vmem_limit_bytes=VMEM_LIMIT_BYTES, flags=flags)


def _const_spec(shape):
    nd = len(shape)
    return pl.BlockSpec(shape, lambda *_: (0,) * nd, pipeline_mode=pl.Buffered(1))


def _rmsnorm(xf, g):
    return xf * lax.rsqrt(jnp.mean(xf * xf, axis=-1, keepdims=True) + RMS_EPS) * g


def _t5_bucket_np(rel):
    half = REL_BUCKETS // 2
    max_exact = half // 2
    ret = np.where(rel > 0, half, 0)
    n = np.abs(rel)
    nf = np.maximum(n, 1).astype(np.float32)
    large = max_exact + (np.log(nf / np.float32(max_exact)) / np.float32(math.log(REL_MAX_DIST / max_exact))
                         * np.float32(half - max_exact)).astype(np.int32)
    large = np.minimum(large, half - 1)
    return (ret + np.where(n < max_exact, n, large)).astype(np.int32)


def _bucket_tiles():
    kk = np.arange(ATT_BK)[:, None]
    qq = np.arange(ATT_BQ)[None, :]
    tiles = []
    for n in range(ATT_NEAR):
        key = kk + (n - 1) * ATT_BK
        tile = _t5_bucket_np(key - qq)
        tiles.append(np.where((key // CHUNK) <= (qq // CHUNK), tile, MASK_BUCKET))
    nearest_far = (-2 * ATT_BK + ATT_BK - 1) - 0
    assert (_t5_bucket_np(np.arange(-8 * ATT_BQ, nearest_far + 1)) == REL_BUCKETS // 2 - 1).all()
    assert (tiles[-1][ATT_BK // 2:, :ATT_BQ // 2] == MASK_BUCKET).all()
    return np.stack(tiles).astype(np.int32)


def _prep_attn_kernel(present, relb_ref, bucket_ref, lq1_ref, lk1_ref, lq2_ref, lk2_ref, bias_ref, lam_ref):
    h = pl.program_id(0)
    for n, buckets in enumerate(present):
        bk = bucket_ref[n]
        acc = jnp.full(bk.shape, MASK_VALUE, F32)
        for b in buckets:
            acc = jnp.where(bk == b, relb_ref[b, h], acc)
        bias_ref[0, n] = (acc - relb_ref[REL_BUCKETS // 2 - 1, h]) * LOG2E
    lam = (jnp.exp(jnp.sum(lq1_ref[...] * lk1_ref[...], axis=-1, keepdims=True))
           - jnp.exp(jnp.sum(lq2_ref[...] * lk2_ref[...], axis=-1, keepdims=True)) + LAM_INIT)
    lam_ref[...] = jnp.broadcast_to(lam, lam_ref.shape)


def _prep_attn(rel_bias, lq1, lk1, lq2, lk2):
    tiles = _bucket_tiles()
    present = tuple(tuple(int(b) for b in np.unique(t) if b != MASK_BUCKET) for t in tiles)
    buckets = jnp.asarray(tiles)
    vec = pl.BlockSpec((1, DA_HEAD_DIM), lambda h: (0, 0))
    return pl.pallas_call(
        functools.partial(_prep_attn_kernel, present),
        grid=(DA_HEADS,),
        in_specs=[pl.BlockSpec(memory_space=pltpu.SMEM),
                  pl.BlockSpec((ATT_NEAR, ATT_BK, ATT_BQ), lambda h: (0, 0, 0)),
                  vec, vec, vec, vec],
        out_specs=[pl.BlockSpec((1, ATT_NEAR, ATT_BK, ATT_BQ), lambda h: (h, 0, 0, 0)),
                   pl.BlockSpec((SUBLANES, LANES), lambda h: (0, 0))],
        out_shape=[jax.ShapeDtypeStruct((DA_HEADS, ATT_NEAR, ATT_BK, ATT_BQ), F32),
                   jax.ShapeDtypeStruct((SUBLANES, LANES), F32)],
        compiler_params=_params(1),
        name="prep_attn",
    )(rel_bias, buckets, lq1, lk1, lq2, lk2)


def _prep_ssm_kernel(are_ref, aim_ref, ldt_ref, bre_ref, bim_ref, pr_ref, pi_ref, wr_ref, wi_ref):
    a_re = are_ref[...]
    a_im = aim_ref[...]
    dt = jnp.exp(ldt_ref[...])
    for n in range(1, SUBLANES + 1):
        mag = jnp.exp(a_re * dt * n)
        ang = a_im * dt * n
        pr_ref[n - 1] = mag * jnp.cos(ang)
        pi_ref[n - 1] = mag * jnp.sin(ang)
    xr = pr_ref[0] - 1.0
    xi = pi_ref[0]
    den = a_re * a_re + a_im * a_im
    cr = ((xr * a_re + xi * a_im) / den)[:, None, :]
    ci = ((xi * a_re - xr * a_im) / den)[:, None, :]
    b_re = bre_ref[...]
    b_im = bim_ref[...]
    bbr = cr * b_re - ci * b_im
    bbi = cr * b_im + ci * b_re
    wr_ref[0] = bbr
    wi_ref[0] = bbi
    for d in range(1, SUBLANES):
        ar = pr_ref[d - 1][:, None, :]
        ai = pi_ref[d - 1][:, None, :]
        wr_ref[d] = ar * bbr - ai * bbi
        wi_ref[d] = ar * bbi + ai * bbr


def _prep_ssm(a_re, a_im, log_dt, b_re, b_im):
    g, p = a_re.shape
    bt_re = jnp.transpose(b_re, (0, 2, 1))
    bt_im = jnp.transpose(b_im, (0, 2, 1))
    pow_shape = jax.ShapeDtypeStruct((SUBLANES, g, p), F32)
    w_shape = jax.ShapeDtypeStruct((SUBLANES, g, SSM_GROUP, p), F32)
    return pl.pallas_call(
        _prep_ssm_kernel,
        out_shape=[pow_shape, pow_shape, w_shape, w_shape],
        name="prep_ssm",
    )(a_re, a_im, log_dt.reshape(g, 1), bt_re, bt_im)


def _ssm_tables(pow_re, pow_im, w_re, w_im, c_re, c_im):
    nb, nh, gh = SSM_LANE_BLOCKS, SSM_HALVES, SSM_GROUPS_PER_HALF
    eye_g = jnp.eye(gh, dtype=bool)
    eye_h = jnp.eye(nh, dtype=bool)

    def w_block(w):
        w = jnp.transpose(w.reshape(SUBLANES, nb, nh, gh, SSM_GROUP, 1, SSM_STATE), (1, 2, 0, 3, 4, 5, 6))
        same_group = eye_g[None, None, None, :, None, :, None]
        return jnp.where(same_group, w, 0.0).reshape(nb, nh, SUBLANES * SSM_HALF_LANES, SSM_HALF_STATES)

    def c_block(cc):
        cc = jnp.transpose(cc.reshape(nb, nh, gh, SSM_GROUP, SSM_STATE), (0, 1, 2, 4, 3))
        cc = cc.reshape(nb, nh, gh, SSM_STATE, 1, 1, SSM_GROUP)
        same = (eye_h[None, :, None, None, :, None, None] & eye_g[None, None, :, None, None, :, None])
        return jnp.where(same, cc, 0.0).reshape(nb, nh, SSM_HALF_STATES, LANES)

    w_blk = jnp.concatenate([w_block(w_re), w_block(w_im)], axis=3).astype(BF16)
    c_blk = jnp.concatenate([c_block(c_re), c_block(-c_im)], axis=2).astype(BF16)

    def lanes(t):
        return jnp.transpose(t.reshape(SUBLANES, nb, nh, SSM_HALF_STATES), (1, 2, 0, 3))

    coef = jnp.stack([lanes(pow_re), lanes(pow_im)], axis=2)
    return w_blk, c_blk, coef


def _proj_kernel(x_ref, g_ref, w_ref, wvt_ref, q_ref, k_ref, u_ref, xq_ref, vt_ref):
    h = _rmsnorm(x_ref[...], g_ref[...]).astype(BF16)
    d = D_MODEL
    q_ref[...] = jnp.dot(h, w_ref[:, 0:d], preferred_element_type=F32).astype(BF16)
    k_ref[...] = jnp.dot(h, w_ref[:, d:2 * d], preferred_element_type=F32).astype(BF16)
    u_ref[...] = jnp.dot(h, w_ref[:, 2 * d:3 * d], preferred_element_type=F32)
    xq_ref[...] = jnp.dot(h, w_ref[:, 3 * d:4 * d], preferred_element_type=F32).astype(BF16)
    vt = lax.dot_general(wvt_ref[...], h, (((1,), (1,)), ((), ())), preferred_element_type=F32).astype(BF16)
    for hd in range(DA_HEADS):
        for jb in range(PROJ_TM // ATT_BK):
            vt_ref[0, hd, jb, 0:DA_V_DIM, :] = vt[hd * DA_V_DIM:(hd + 1) * DA_V_DIM,
                                                   jb * ATT_BK:(jb + 1) * ATT_BK]
            tail_row = lax.broadcasted_iota(jnp.int32, (ATT_V_ROWS - DA_V_DIM, ATT_BK), 0)
            vt_ref[0, hd, jb, DA_V_DIM:ATT_V_ROWS, :] = jnp.where(tail_row == 0, 1.0, 0.0).astype(BF16)


def _proj(x2d, g, w4, wvt, batch, seq):
    n = x2d.shape[0]
    tiles_per_batch = seq // PROJ_TM
    blocks_per_tile = PROJ_TM // ATT_BK
    row = pl.BlockSpec((PROJ_TM, D_MODEL), lambda i: (i, 0))
    act = lambda dt: jax.ShapeDtypeStruct((n, D_MODEL), dt)
    return pl.pallas_call(
        _proj_kernel,
        grid=(n // PROJ_TM,),
        in_specs=[row, _const_spec((1, D_MODEL)), _const_spec((D_MODEL, 4 * D_MODEL)),
                  _const_spec((D_MODEL, D_MODEL))],
        out_specs=[row, row, row, row,
                   pl.BlockSpec((1, DA_HEADS, blocks_per_tile, ATT_V_ROWS, ATT_BK),
                                lambda i: (i // tiles_per_batch, 0, i % tiles_per_batch, 0, 0))],
        out_shape=[act(BF16), act(BF16), act(F32), act(BF16),
                   jax.ShapeDtypeStruct((batch, DA_HEADS, seq // ATT_BK, ATT_V_ROWS, ATT_BK), BF16)],
        compiler_params=_params(1),
        name="proj",
    )(x2d, g, w4, wvt)


def _attn_kernel(lam_ref, q_ref, k_ref, vt_ref, bias_ref, g_ref, o_ref, m_sc, acc_sc, s_sc, bm_sc, p_sc, a_sc):
    ratio = ATT_BQ // ATT_BK
    lam = lam_ref[0, 0]
    gain = g_ref[...] * (1.0 - LAM_INIT)
    lane = lax.broadcasted_iota(jnp.int32, (ATT_BQ, 2 * DA_HEAD_DIM), 1)

    assert ratio == 1
    half = ATT_BK // 2
    nt = (((1,), (1,)), ((), ()))

    def scores(qs, j, tile, diag):
        kb = k_ref[0, j * ATT_BK:(j + 1) * ATT_BK, :]
        for c in range(2):
            if not diag:
                s = lax.dot_general(kb, qs[c], nt, preferred_element_type=F32)
                if tile is not None:
                    s = s + bias_ref[0, tile]
                s_sc[c] = s
                bm_sc[c] = jnp.max(s, axis=0, keepdims=True)
            else:
                lo = lax.dot_general(kb[0:half], qs[c], nt, preferred_element_type=F32)
                lo = lo + bias_ref[0, tile, 0:half, :]
                s_sc[c, 0:half, :] = lo
                m_lo = jnp.max(lo, axis=0, keepdims=True)
                hi = lax.dot_general(kb[half:], qs[c][half:], nt, preferred_element_type=F32)
                hi = hi + bias_ref[0, tile, half:, half:]
                s_sc[c, half:, half:] = hi
                m_hi = jnp.max(hi, axis=0, keepdims=True)
                bm_sc[c] = jnp.concatenate([m_lo[:, :half], jnp.maximum(m_lo[:, half:], m_hi)], axis=1)

    def values(j, diag):
        vtb = vt_ref[0, 0, j]
        for c in range(2):
            if not diag:
                pv = jnp.dot(vtb, p_sc[c], preferred_element_type=F32)
            else:
                lo = jnp.dot(vtb[:, 0:half], p_sc[c, 0:half, :], preferred_element_type=F32)
                hi = jnp.dot(vtb[:, half:], p_sc[c, half:, half:], preferred_element_type=F32)
                pv = jnp.concatenate([lo[:, :half], lo[:, half:] + hi], axis=1)
            acc_sc[c] = pv if j == 0 else a_sc[c] * acc_sc[c] + pv

    def softmax(first, diag):
        for c in range(2):
            m_new = bm_sc[c]
            if not first:
                m_old = m_sc[c]
                m_new = jnp.maximum(m_old, m_new)
                a_sc[c] = jnp.exp2(m_old - m_new)
            m_sc[c] = m_new
            if not diag:
                p_sc[c] = jnp.exp2(s_sc[c] - m_new).astype(BF16)
            else:
                p_sc[c, 0:half, :] = jnp.exp2(s_sc[c, 0:half, :] - m_new).astype(BF16)
                p_sc[c, half:, half:] = jnp.exp2(s_sc[c, half:, half:] - m_new[:, half:]).astype(BF16)

    def finish(i):
        num = [acc_sc[c, 0:DA_V_DIM, :] for c in range(2)]
        den = [acc_sc[c, DA_V_DIM:DA_V_DIM + 1, :] for c in range(2)]
        o = num[0] * (1.0 / den[0]) - lam * (num[1] * (1.0 / den[1]))
        o = o * lax.rsqrt(jnp.mean(o * o, axis=0, keepdims=True) + RMS_EPS)
        o_ref[0, i * ATT_BQ:(i + 1) * ATT_BQ, :] = (o * gain).T.astype(BF16)

    for i in range(q_ref.shape[1] // ATT_BQ):
        qh = q_ref[0, i * ATT_BQ:(i + 1) * ATT_BQ, :]
        zero = jnp.zeros_like(qh)
        qs = (jnp.where(lane < DA_HEAD_DIM, qh, zero), jnp.where(lane >= DA_HEAD_DIM, qh, zero))
        first_near = ratio * i - 1
        last = first_near + ATT_NEAR - 1

        def tile_of(j, first_near=first_near):
            return j - first_near if 0 <= j - first_near < ATT_NEAR else None

        scores(qs, 0, tile_of(0), diag=(last == 0))
        for t in range(last + 1):
            if t >= 1:
                values(t - 1, diag=False)
            softmax(first=(t == 0), diag=(t == last))
            if t < last:
                scores(qs, t + 1, tile_of(t + 1), diag=(t + 1 == last))
        values(last, diag=True)
        finish(i)


def _attention(lam, q, k, vt, bias, g_col, batch, seq):
    bq, bk = ATT_BQ, ATT_BK
    whole = pl.BlockSpec((1, seq, DA_V_DIM), lambda b, h: (b, 0, h))
    return pl.pallas_call(
        _attn_kernel,
        grid=(batch, DA_HEADS),
        in_specs=[pl.BlockSpec(memory_space=pltpu.SMEM),
                  whole,
                  whole,
                  pl.BlockSpec((1, 1, seq // bk, ATT_V_ROWS, bk), lambda b, h: (b, h, 0, 0, 0)),
                  pl.BlockSpec((1, ATT_NEAR, bk, bq), lambda b, h: (h, 0, 0, 0)),
                  pl.BlockSpec((DA_V_DIM, 1), lambda b, h: (0, 0))],
        out_specs=whole,
        out_shape=jax.ShapeDtypeStruct((batch, seq, D_MODEL), BF16),
        scratch_shapes=[pltpu.VMEM((2, 1, bq), F32),
                        pltpu.VMEM((2, ATT_V_ROWS, bq), F32),
                        pltpu.VMEM((2, bk, bq), F32),
                        pltpu.VMEM((2, 1, bq), F32),
                        pltpu.VMEM((2, bk, bq), BF16),
                        pltpu.VMEM((2, 1, bq), F32)],
        compiler_params=_params(2),
        name="attention",
    )(lam, q, k, vt, bias, g_col)


def _ssm_kernel(u_ref, w_ref, c_ref, coef_ref, d_ref, y_ref, h_sc):
    t = pl.program_id(2)
    ns = SSM_HALF_STATES

    @pl.when(t == 0)
    def _():
        h_sc[...] = jnp.zeros(h_sc.shape, F32)

    rows = SSM_ROWS
    groups = rows // SUBLANES
    halves = range(SSM_HALVES)

    def local_states(blk):
        u3 = u_ref[0, blk * rows:(blk + 1) * rows, :].reshape(groups, SUBLANES, LANES)
        row = lax.broadcasted_iota(jnp.int32, u3.shape, 1)
        low = lax.broadcasted_iota(jnp.int32, u3.shape, 2) < SSM_HALF_LANES
        delayed = [u3] + [jnp.where(row >= d, pltpu.roll(u3, d, axis=1), 0.0) for d in range(1, SUBLANES)]
        swapped = [pltpu.roll(ud, SSM_HALF_LANES, axis=2) for ud in delayed]
        out = []
        for h in halves:
            pieces = []
            for d in range(0, SUBLANES, 2):
                pair = (jnp.where(low, delayed[d], swapped[d + 1]) if h == 0
                        else jnp.where(low, swapped[d], delayed[d + 1]))
                pieces.append(pair.reshape(rows, LANES).astype(BF16))
            out.append(jnp.dot(jnp.concatenate(pieces, axis=1), w_ref[0, h], preferred_element_type=F32))
        return out

    def carry_and_project(blk, local, state):
        y = None
        new_state = []
        for h in halves:
            pr = coef_ref[0, h, 0]
            pi = coef_ref[0, h, 1]
            hr, hi = state[h]
            xr = local[h][:, :ns].reshape(groups, SUBLANES, ns)
            xi = local[h][:, ns:].reshape(groups, SUBLANES, ns)
            out_r, out_i = [], []
            for g in range(groups):
                gr = xr[g] + (pr * hr - pi * hi)
                gi = xi[g] + (pr * hi + pi * hr)
                hr = gr[SUBLANES - 1:SUBLANES]
                hi = gi[SUBLANES - 1:SUBLANES]
                out_r.append(gr)
                out_i.append(gi)
            new_state.append((hr, hi))
            x_all = jnp.concatenate([jnp.concatenate(out_r, axis=0), jnp.concatenate(out_i, axis=0)], axis=1)
            part = jnp.dot(x_all.astype(BF16), c_ref[0, h], preferred_element_type=F32)
            y = part if y is None else y + part
        sl = slice(blk * rows, (blk + 1) * rows)
        y_ref[0, sl, :] = y + d_ref[...] * u_ref[0, sl, :]
        return new_state

    n_blk = SSM_T // rows
    state = [(h_sc[h, 0], h_sc[h, 1]) for h in halves]
    pending = local_states(0)
    for blk in range(n_blk):
        upcoming = local_states(blk + 1) if blk + 1 < n_blk else None
        state = carry_and_project(blk, pending, state)
        pending = upcoming
    for h in halves:
        h_sc[h, 0] = state[h][0]
        h_sc[h, 1] = state[h][1]


def _ssm(u, w_blk, c_blk, coef, d_skip, batch, seq):
    nb = SSM_LANE_BLOCKS
    act = pl.BlockSpec((1, SSM_T, LANES), lambda b, j, t: (b, t, j))
    return pl.pallas_call(
        _ssm_kernel,
        grid=(batch, nb, seq // SSM_T),
        in_specs=[act,
                  pl.BlockSpec((1, SSM_HALVES, SUBLANES * SSM_HALF_LANES, 2 * SSM_HALF_STATES),
                               lambda b, j, t: (j, 0, 0, 0)),
                  pl.BlockSpec((1, SSM_HALVES, 2 * SSM_HALF_STATES, LANES), lambda b, j, t: (j, 0, 0, 0)),
                  pl.BlockSpec((1, SSM_HALVES, 2, SUBLANES, SSM_HALF_STATES), lambda b, j, t: (j, 0, 0, 0, 0)),
                  pl.BlockSpec((1, LANES), lambda b, j, t: (0, j))],
        out_specs=act,
        out_shape=jax.ShapeDtypeStruct((batch, seq, D_MODEL), F32),
        scratch_shapes=[pltpu.VMEM((SSM_HALVES, 2, 1, SSM_HALF_STATES), F32)],
        compiler_params=_params(3),
        name="ssm",
    )(u, w_blk, c_blk, coef, d_skip)


def _mem_kv_kernel(mem_ref, g_ref, w_ref, k_ref, v_ref):
    mn = _rmsnorm(mem_ref[0], g_ref[...]).astype(BF16)
    kv = jnp.dot(mn, w_ref[...], preferred_element_type=F32)
    k_ref[0] = kv[:, :D_MODEL].astype(BF16)
    v_ref[0] = kv[:, D_MODEL:].astype(BF16)


def _mem_kv(mem, g, w):
    batch, m, _ = mem.shape
    blk = pl.BlockSpec((1, m, D_MODEL), lambda b: (b, 0, 0))
    shape = jax.ShapeDtypeStruct((batch, m, D_MODEL), BF16)
    return pl.pallas_call(
        _mem_kv_kernel,
        grid=(batch,),
        in_specs=[blk, _const_spec((1, D_MODEL)), _const_spec((D_MODEL, 2 * D_MODEL))],
        out_specs=[blk, blk],
        out_shape=[shape, shape],
        compiler_params=_params(1),
        name="mem_kv",
    )(mem, g, w)


def _gelu_tanh(x):
    return 0.5 * x * (1.0 + jnp.tanh(math.sqrt(2.0 / math.pi) * (x + 0.044715 * (x * x * x))))


def _sigmoid(x):
    return 1.0 / (1.0 + jnp.exp(-x))


def _merge_kernel(x_ref, ya_ref, ys_ref, xq_ref, km_ref, vm_ref, g1_ref, wg_ref, glu_w_ref, glu_b_ref,
                  wa_ref, ws_ref, wx_ref, wo_ref, o_ref):
    d = D_MODEL
    x = x_ref[...]
    h = _rmsnorm(x, g1_ref[...]).astype(BF16)

    mixed = _sigmoid(jnp.dot(h, wg_ref[:, 0:d], preferred_element_type=F32)) * jnp.dot(
        ya_ref[...], wa_ref[...], preferred_element_type=F32)

    z = _gelu_tanh(ys_ref[...])
    zb = z.astype(BF16)
    y_ssm = z * _sigmoid(jnp.dot(zb, glu_w_ref[...], preferred_element_type=F32) + glu_b_ref[...])
    mixed = mixed + _sigmoid(jnp.dot(h, wg_ref[:, d:2 * d], preferred_element_type=F32)) * jnp.dot(
        y_ssm.astype(BF16), ws_ref[...], preferred_element_type=F32)

    heads = []
    for hd in range(XA_HEADS):
        sl = slice(hd * XA_HEAD_DIM, (hd + 1) * XA_HEAD_DIM)
        s = lax.dot_general(xq_ref[:, sl], km_ref[0, :, sl], (((1,), (1,)), ((), ())),
                            preferred_element_type=F32) * (XA_HEAD_DIM ** -0.5)
        p = jnp.exp(s - jnp.max(s, axis=-1, keepdims=True))
        p = p * (1.0 / jnp.sum(p, axis=-1, keepdims=True))
        heads.append(jnp.dot(p.astype(BF16), vm_ref[0, :, sl], preferred_element_type=F32))
    y_x = jnp.concatenate(heads, axis=1).astype(BF16)
    mixed = mixed + _sigmoid(jnp.dot(h, wg_ref[:, 2 * d:3 * d], preferred_element_type=F32)) * jnp.dot(
        y_x, wx_ref[...], preferred_element_type=F32)

    o_ref[...] = x + jnp.dot(mixed.astype(BF16), wo_ref[...], preferred_element_type=F32)


def _merge(x2d, y_attn, y_s, xq, kmem, vmem, g1, wg, glu_w, glu_b, wa, ws, wx, wo, seq):
    n = x2d.shape[0]
    tiles_per_batch = seq // MERGE_TM
    m = kmem.shape[1]
    row = pl.BlockSpec((MERGE_TM, D_MODEL), lambda i: (i, 0))
    mem = pl.BlockSpec((1, m, D_MODEL), lambda i: (i // tiles_per_batch, 0, 0))
    sq = _const_spec((D_MODEL, D_MODEL))
    vec = _const_spec((1, D_MODEL))
    return pl.pallas_call(
        _merge_kernel,
        grid=(n // MERGE_TM,),
        in_specs=[row, row, row, row, mem, mem, vec, _const_spec((D_MODEL, 3 * D_MODEL)), sq, vec,
                  sq, sq, sq, sq],
        out_specs=row,
        out_shape=jax.ShapeDtypeStruct((n, D_MODEL), F32),
        compiler_params=_params(1),
        name="merge",
    )(x2d, y_attn, y_s, xq, kmem, vmem, g1, wg, glu_w, glu_b, wa, ws, wx, wo)


def _ffn_kernel(x_ref, g2_ref, wi_ref, wo_ref, gf_ref, o_ref):
    x = x_ref[...]
    h = _rmsnorm(x, g2_ref[...]).astype(BF16)
    acc = x
    for c in range(FFN_HIDDEN // FFN_CHUNK):
        lo = c * FFN_CHUNK
        gate = jnp.dot(h, wi_ref[:, lo:lo + FFN_CHUNK], preferred_element_type=F32)
        up = jnp.dot(h, wi_ref[:, FFN_HIDDEN + lo:FFN_HIDDEN + lo + FFN_CHUNK], preferred_element_type=F32)
        act = (gate * _sigmoid(gate) * up).astype(BF16)
        acc = acc + jnp.dot(act, wo_ref[lo:lo + FFN_CHUNK, :], preferred_element_type=F32)
    o_ref[...] = _rmsnorm(acc, gf_ref[...])


def _ffn(x2d, g2, wi, wo, gf):
    n = x2d.shape[0]
    row = pl.BlockSpec((FFN_TM, D_MODEL), lambda i: (i, 0))
    vec = _const_spec((1, D_MODEL))
    return pl.pallas_call(
        _ffn_kernel,
        grid=(n // FFN_TM,),
        in_specs=[row, vec, _const_spec((D_MODEL, 2 * FFN_HIDDEN)), _const_spec((FFN_HIDDEN, D_MODEL)), vec],
        out_specs=row,
        out_shape=jax.ShapeDtypeStruct((n, D_MODEL), F32),
        compiler_params=_params(1),
        name="ffn",
    )(x2d, g2, wi, wo, gf)


def kernel(x, mem, norm1_g, w_in, da_lq1, da_lk1, da_lq2, da_lk2, da_subln_g, rel_bias, ssm_a_re, ssm_a_im, ssm_log_dt, ssm_b_re, ssm_b_im, ssm_c_re, ssm_c_im, ssm_d, glu_w, glu_b, mem_norm_g, w_mem_kv, w_br_attn, w_br_ssm, w_br_xattn, w_out, norm2_g, w_ffn_in, w_ffn_out, final_g):
    batch, seq, d = x.shape
    depth = w_in.shape[0]
    assert depth == 1 and d == D_MODEL and seq % PROJ_TM == 0
    layer = 0
    x2d = x.reshape(batch * seq, d)
    row = lambda v: v.reshape(1, -1).astype(F32)

    w = w_in[layer]
    scale = DA_HEAD_DIM ** -0.5 * LOG2E
    w4 = jnp.concatenate([w[:, 0:d] * scale, w[:, d:2 * d], w[:, 3 * d:4 * d], w[:, 4 * d:5 * d]],
                         axis=1).astype(BF16)
    wvt = jnp.transpose(w[:, 2 * d:3 * d]).astype(BF16)
    wg = w[:, 5 * d:8 * d].astype(BF16)

    bias, lam_tile = _prep_attn(rel_bias, row(da_lq1[layer]), row(da_lk1[layer]), row(da_lq2[layer]),
                                row(da_lk2[layer]))
    lam = lam_tile[0:1, 0:1]

    pow_re, pow_im, w_re, w_im = _prep_ssm(ssm_a_re[layer], ssm_a_im[layer], ssm_log_dt[layer],
                                           ssm_b_re[layer], ssm_b_im[layer])
    w_blk, c_blk, coef = _ssm_tables(pow_re, pow_im, w_re, w_im, ssm_c_re[layer], ssm_c_im[layer])

    q, k, u, xq, vt = _proj(x2d, row(norm1_g[layer]), w4, wvt, batch, seq)
    y_attn = _attention(lam, q.reshape(batch, seq, d), k.reshape(batch, seq, d), vt, bias,
                        da_subln_g[layer].reshape(DA_V_DIM, 1).astype(F32), batch, seq)
    y_s = _ssm(u.reshape(batch, seq, d), w_blk, c_blk, coef, row(ssm_d[layer]), batch, seq)
    kmem, vmem = _mem_kv(mem, row(mem_norm_g[layer]), w_mem_kv[layer].astype(BF16))
    x_mid = _merge(x2d, y_attn.reshape(batch * seq, d), y_s.reshape(batch * seq, d), xq, kmem, vmem,
                   row(norm1_g[layer]), wg, glu_w[layer].astype(BF16), row(glu_b[layer]),
                   w_br_attn[layer].astype(BF16), w_br_ssm[layer].astype(BF16), w_br_xattn[layer].astype(BF16),
                   w_out[layer].astype(BF16), seq)
    out = _ffn(x_mid, row(norm2_g[layer]), w_ffn_in[layer].astype(BF16), w_ffn_out[layer].astype(BF16),
               row(final_g))
    return out.reshape(batch, seq, d)
```

```python
import functools
import math

import jax
import jax.numpy as jnp
import numpy as np
from jax import lax
from jax.experimental import pallas as pl
from jax.experimental.pallas import tpu as pltpu

F32 = jnp.float32
BF16 = jnp.bfloat16

D_MODEL = 1024
CHUNK = 64
DA_HEADS = 8
DA_HEAD_DIM = 64
DA_V_DIM = 128
SSM_GROUP = 16
SSM_GROUPS = 64
SSM_STATE = 64
XA_HEADS = 4
XA_HEAD_DIM = 256
REL_BUCKETS = 32
REL_MAX_DIST = 256
FFN_HIDDEN = 2816
RMS_EPS = 1e-6
LAM_INIT = 0.8 - 0.6 * math.exp(-0.3 * 0)

LANES = 128
SUBLANES = 8
VMEM_LIMIT_BYTES = 56 * 1024 * 1024

ATT_BQ = 512
ATT_BK = 512
ATT_NEAR = ATT_BQ // ATT_BK + 1
ATT_V_ROWS = DA_V_DIM + 16
LOG2E = math.log2(math.e)
PROJ_TM = 512
SSM_T = 2048
SSM_ROWS = 256
SSM_LANE_BLOCKS = D_MODEL // LANES
SSM_HALVES = 2
SSM_HALF_LANES = LANES // SSM_HALVES
SSM_GROUPS_PER_HALF = SSM_HALF_LANES // SSM_GROUP
SSM_HALF_STATES = SSM_GROUPS_PER_HALF * SSM_STATE
MERGE_TM = 512
FFN_TM = 1024
FFN_CHUNK = 256
MASK_VALUE = -1e30
MASK_BUCKET = REL_BUCKETS


def _params(n_axes, flags=None):
    return pltpu.CompilerParams(dimension_semantics=("arbitrary",) * n_axes,
                                vmem_limit_bytes=VMEM_LIMIT_BYTES, flags=flags)


def _const_spec(shape):
    nd = len(shape)
    return pl.BlockSpec(shape, lambda *_: (0,) * nd, pipeline_mode=pl.Buffered(1))


def _rmsnorm(xf, g):
    return xf * lax.rsqrt(jnp.mean(xf * xf, axis=-1, keepdims=True) + RMS_EPS) * g


def _t5_bucket_np(rel):
    half = REL_BUCKETS // 2
    max_exact = half // 2
    ret = np.where(rel > 0, half, 0)
    n = np.abs(rel)
    nf = np.maximum(n, 1).astype(np.float32)
    large = max_exact + (np.log(nf / np.float32(max_exact)) / np.float32(math.log(REL_MAX_DIST / max_exact))
                         * np.float32(half - max_exact)).astype(np.int32)
    large = np.minimum(large, half - 1)
    return (ret + np.where(n < max_exact, n, large)).astype(np.int32)


def _bucket_tiles():
    kk = np.arange(ATT_BK)[:, None]
    qq = np.arange(ATT_BQ)[None, :]
    tiles = []
    for n in range(ATT_NEAR):
        key = kk + (n - 1) * ATT_BK
        tile = _t5_bucket_np(key - qq)
        tiles.append(np.where((key // CHUNK) <= (qq // CHUNK), tile, MASK_BUCKET))
    nearest_far = (-2 * ATT_BK + ATT_BK - 1) - 0
    assert (_t5_bucket_np(np.arange(-8 * ATT_BQ, nearest_far + 1)) == REL_BUCKETS // 2 - 1).all()
    assert (tiles[-1][ATT_BK // 2:, :ATT_BQ // 2] == MASK_BUCKET).all()
    return np.stack(tiles).astype(np.int32)


def _prep_attn_kernel(present, relb_ref, bucket_ref, lq1_ref, lk1_ref, lq2_ref, lk2_ref, bias_ref, lam_ref):
    h = pl.program_id(0)
    for n, buckets in enumerate(present):
        bk = bucket_ref[n]
        acc = jnp.full(bk.shape, MASK_VALUE, F32)
        for b in buckets:
            acc = jnp.where(bk == b, relb_ref[b, h], acc)
        bias_ref[0, n] = (acc - relb_ref[REL_BUCKETS // 2 - 1, h]) * LOG2E
    lam = (jnp.exp(jnp.sum(lq1_ref[...] * lk1_ref[...], axis=-1, keepdims=True))
           - jnp.exp(jnp.sum(lq2_ref[...] * lk2_ref[...], axis=-1, keepdims=True)) + LAM_INIT)
    lam_ref[...] = jnp.broadcast_to(lam, lam_ref.shape)


def _prep_attn(rel_bias, lq1, lk1, lq2, lk2):
    tiles = _bucket_tiles()
    present = tuple(tuple(int(b) for b in np.unique(t) if b != MASK_BUCKET) for t in tiles)
    buckets = jnp.asarray(tiles)
    vec = pl.BlockSpec((1, DA_HEAD_DIM), lambda h: (0, 0))
    return pl.pallas_call(
        functools.partial(_prep_attn_kernel, present),
        grid=(DA_HEADS,),
        in_specs=[pl.BlockSpec(memory_space=pltpu.SMEM),
                  pl.BlockSpec((ATT_NEAR, ATT_BK, ATT_BQ), lambda h: (0, 0, 0)),
                  vec, vec, vec, vec],
        out_specs=[pl.BlockSpec((1, ATT_NEAR, ATT_BK, ATT_BQ), lambda h: (h, 0, 0, 0)),
                   pl.BlockSpec((SUBLANES, LANES), lambda h: (0, 0))],
        out_shape=[jax.ShapeDtypeStruct((DA_HEADS, ATT_NEAR, ATT_BK, ATT_BQ), F32),
                   jax.ShapeDtypeStruct((SUBLANES, LANES), F32)],
        compiler_params=_params(1),
        name="prep_attn",
    )(rel_bias, buckets, lq1, lk1, lq2, lk2)


def _prep_ssm_kernel(are_ref, aim_ref, ldt_ref, bre_ref, bim_ref, pr_ref, pi_ref, wr_ref, wi_ref):
    a_re = are_ref[...]
    a_im = aim_ref[...]
    dt = jnp.exp(ldt_ref[...])
    for n in range(1, SUBLANES + 1):
        mag = jnp.exp(a_re * dt * n)
        ang = a_im * dt * n
        pr_ref[n - 1] = mag * jnp.cos(ang)
        pi_ref[n - 1] = mag * jnp.sin(ang)
    xr = pr_ref[0] - 1.0
    xi = pi_ref[0]
    den = a_re * a_re + a_im * a_im
    cr = ((xr * a_re + xi * a_im) / den)[:, None, :]
    ci = ((xi * a_re - xr * a_im) / den)[:, None, :]
    b_re = bre_ref[...]
    b_im = bim_ref[...]
    bbr = cr * b_re - ci * b_im
    bbi = cr * b_im + ci * b_re
    wr_ref[0] = bbr
    wi_ref[0] = bbi
    for d in range(1, SUBLANES):
        ar = pr_ref[d - 1][:, None, :]
        ai = pi_ref[d - 1][:, None, :]
        wr_ref[d] = ar * bbr - ai * bbi
        wi_ref[d] = ar * bbi + ai * bbr


def _prep_ssm(a_re, a_im, log_dt, b_re, b_im):
    g, p = a_re.shape
    bt_re = jnp.transpose(b_re, (0, 2, 1))
    bt_im = jnp.transpose(b_im, (0, 2, 1))
    pow_shape = jax.ShapeDtypeStruct((SUBLANES, g, p), F32)
    w_shape = jax.ShapeDtypeStruct((SUBLANES, g, SSM_GROUP, p), F32)
    return pl.pallas_call(
        _prep_ssm_kernel,
        out_shape=[pow_shape, pow_shape, w_shape, w_shape],
        name="prep_ssm",
    )(a_re, a_im, log_dt.reshape(g, 1), bt_re, bt_im)


def _ssm_tables(pow_re, pow_im, w_re, w_im, c_re, c_im):
    nb, nh, gh = SSM_LANE_BLOCKS, SSM_HALVES, SSM_GROUPS_PER_HALF
    eye_g = jnp.eye(gh, dtype=bool)
    eye_h = jnp.eye(nh, dtype=bool)

    def w_block(w):
        w = jnp.transpose(w.reshape(SUBLANES, nb, nh, gh, SSM_GROUP, 1, SSM_STATE), (1, 2, 0, 3, 4, 5, 6))
        same_group = eye_g[None, None, None, :, None, :, None]
        return jnp.where(same_group, w, 0.0).reshape(nb, nh, SUBLANES * SSM_HALF_LANES, SSM_HALF_STATES)

    def c_block(cc):
        cc = jnp.transpose(cc.reshape(nb, nh, gh, SSM_GROUP, SSM_STATE), (0, 1, 2, 4, 3))
        cc = cc.reshape(nb, nh, gh, SSM_STATE, 1, 1, SSM_GROUP)
        same = (eye_h[None, :, None, None, :, None, None] & eye_g[None, None, :, None, None, :, None])
        return jnp.where(same, cc, 0.0).reshape(nb, nh, SSM_HALF_STATES, LANES)

    w_blk = jnp.concatenate([w_block(w_re), w_block(w_im)], axis=3).astype(BF16)
    c_blk = jnp.concatenate([c_block(c_re), c_block(-c_im)], axis=2).astype(BF16)

    def lanes(t):
        return jnp.transpose(t.reshape(SUBLANES, nb, nh, SSM_HALF_STATES), (1, 2, 0, 3))

    coef = jnp.stack([lanes(pow_re), lanes(pow_im)], axis=2)
    return w_blk, c_blk, coef


def _proj_kernel(x_ref, g_ref, w_ref, wvt_ref, q_ref, k_ref, u_ref, xq_ref, vt_ref):
    h = _rmsnorm(x_ref[...], g_ref[...]).astype(BF16)
    d = D_MODEL
    q_ref[...] = jnp.dot(h, w_ref[:, 0:d], preferred_element_type=F32).astype(BF16)
    k_ref[...] = jnp.dot(h, w_ref[:, d:2 * d], preferred_element_type=F32).astype(BF16)
    u_ref[...] = jnp.dot(h, w_ref[:, 2 * d:3 * d], preferred_element_type=F32)
    xq_ref[...] = jnp.dot(h, w_ref[:, 3 * d:4 * d], preferred_element_type=F32).astype(BF16)
    vt = lax.dot_general(wvt_ref[...], h, (((1,), (1,)), ((), ())), preferred_element_type=F32).astype(BF16)
    for hd in range(DA_HEADS):
        for jb in range(PROJ_TM // ATT_BK):
            vt_ref[0, hd, jb, 0:DA_V_DIM, :] = vt[hd * DA_V_DIM:(hd + 1) * DA_V_DIM,
                                                   jb * ATT_BK:(jb + 1) * ATT_BK]
            tail_row = lax.broadcasted_iota(jnp.int32, (ATT_V_ROWS - DA_V_DIM, ATT_BK), 0)
            vt_ref[0, hd, jb, DA_V_DIM:ATT_V_ROWS, :] = jnp.where(tail_row == 0, 1.0, 0.0).astype(BF16)


def _proj(x2d, g, w4, wvt, batch, seq):
    n = x2d.shape[0]
    tiles_per_batch = seq // PROJ_TM
    blocks_per_tile = PROJ_TM // ATT_BK
    row = pl.BlockSpec((PROJ_TM, D_MODEL), lambda i: (i, 0))
    act = lambda dt: jax.ShapeDtypeStruct((n, D_MODEL), dt)
    return pl.pallas_call(
        _proj_kernel,
        grid=(n // PROJ_TM,),
        in_specs=[row, _const_spec((1, D_MODEL)), _const_spec((D_MODEL, 4 * D_MODEL)),
                  _const_spec((D_MODEL, D_MODEL))],
        out_specs=[row, row, row, row,
                   pl.BlockSpec((1, DA_HEADS, blocks_per_tile, ATT_V_ROWS, ATT_BK),
                                lambda i: (i // tiles_per_batch, 0, i % tiles_per_batch, 0, 0))],
        out_shape=[act(BF16), act(BF16), act(F32), act(BF16),
                   jax.ShapeDtypeStruct((batch, DA_HEADS, seq // ATT_BK, ATT_V_ROWS, ATT_BK), BF16)],
        compiler_params=_params(1),
        name="proj",
    )(x2d, g, w4, wvt)


def _attn_kernel(lam_ref, q_ref, k_ref, vt_ref, bias_ref, g_ref, o_ref):
    ratio = ATT_BQ // ATT_BK
    lam = lam_ref[0, 0]
    gain = g_ref[...] * (1.0 - LAM_INIT)
    lane = lax.broadcasted_iota(jnp.int32, (ATT_BQ, 2 * DA_HEAD_DIM), 1)

    assert ratio == 1
    half = ATT_BK // 2
    nt = (((1,), (1,)), ((), ()))

    def scores(qs, j, tile, diag):
        kb = k_ref[0, j * ATT_BK:(j + 1) * ATT_BK, :]
        out = []
        for c in range(2):
            if not diag:
                s = lax.dot_general(kb, qs[c], nt, preferred_element_type=F32)
                if tile is not None:
                    s = s + bias_ref[0, tile]
                out.append(((s,), jnp.max(s, axis=0, keepdims=True)))
            else:
                lo = lax.dot_general(kb[0:half], qs[c], nt, preferred_element_type=F32)
                lo = lo + bias_ref[0, tile, 0:half, :]
                m_lo = jnp.max(lo, axis=0, keepdims=True)
                hi = lax.dot_general(kb[half:], qs[c][half:], nt, preferred_element_type=F32)
                hi = hi + bias_ref[0, tile, half:, half:]
                m_hi = jnp.max(hi, axis=0, keepdims=True)
                col_max = jnp.concatenate([m_lo[:, :half], jnp.maximum(m_lo[:, half:], m_hi)], axis=1)
                out.append(((lo, hi), col_max))
        return out

    def softmax(sc, m_prev, diag):
        probs, m_next, alpha = [], [], []
        for c in range(2):
            parts, m_new = sc[c]
            if m_prev is None:
                alpha.append(None)
            else:
                m_new = jnp.maximum(m_prev[c], m_new)
                alpha.append(jnp.exp2(m_prev[c] - m_new))
            m_next.append(m_new)
            if not diag:
                probs.append((jnp.exp2(parts[0] - m_new).astype(BF16),))
            else:
                probs.append((jnp.exp2(parts[0] - m_new).astype(BF16),
                              jnp.exp2(parts[1] - m_new[:, half:]).astype(BF16)))
        return probs, m_next, alpha

    def values(j, probs, alpha, acc, diag):
        vtb = vt_ref[0, 0, j]
        out = []
        for c in range(2):
            if not diag:
                pv = jnp.dot(vtb, probs[c][0], preferred_element_type=F32)
            else:
                lo = jnp.dot(vtb[:, 0:half], probs[c][0], preferred_element_type=F32)
                hi = jnp.dot(vtb[:, half:], probs[c][1], preferred_element_type=F32)
                pv = jnp.concatenate([lo[:, :half], lo[:, half:] + hi], axis=1)
            out.append(pv if acc is None else alpha[c] * acc[c] + pv)
        return out

    def finish(i, acc):
        num = [acc[c][0:DA_V_DIM, :] for c in range(2)]
        den = [acc[c][DA_V_DIM:DA_V_DIM + 1, :] for c in range(2)]
        o = num[0] * (1.0 / den[0]) - lam * (num[1] * (1.0 / den[1]))
        o = o * lax.rsqrt(jnp.mean(o * o, axis=0, keepdims=True) + RMS_EPS)
        o_ref[0, i * ATT_BQ:(i + 1) * ATT_BQ, :] = (o * gain).T.astype(BF16)

    for i in range(q_ref.shape[1] // ATT_BQ):
        qh = q_ref[0, i * ATT_BQ:(i + 1) * ATT_BQ, :]
        zero = jnp.zeros_like(qh)
        qs = (jnp.where(lane < DA_HEAD_DIM, qh, zero), jnp.where(lane >= DA_HEAD_DIM, qh, zero))
        first_near = ratio * i - 1
        last = first_near + ATT_NEAR - 1

        def tile_of(j, first_near=first_near):
            return j - first_near if 0 <= j - first_near < ATT_NEAR else None

        sc = scores(qs, 0, tile_of(0), diag=(last == 0))
        m_run, acc, pending = None, None, None
        for t in range(last + 1):
            if pending is not None:
                acc = values(t - 1, pending[0], pending[1], acc, diag=False)
            probs, m_run, alpha = softmax(sc, m_run, diag=(t == last))
            pending = (probs, alpha)
            if t < last:
                sc = scores(qs, t + 1, tile_of(t + 1), diag=(t + 1 == last))
        acc = values(last, pending[0], pending[1], acc, diag=True)
        finish(i, acc)


def _attention(lam, q, k, vt, bias, g_col, batch, seq):
    bq, bk = ATT_BQ, ATT_BK
    whole = pl.BlockSpec((1, seq, DA_V_DIM), lambda b, h: (b, 0, h))
    return pl.pallas_call(
        _attn_kernel,
        grid=(batch, DA_HEADS),
        in_specs=[pl.BlockSpec(memory_space=pltpu.SMEM),
                  whole,
                  whole,
                  pl.BlockSpec((1, 1, seq // bk, ATT_V_ROWS, bk), lambda b, h: (b, h, 0, 0, 0)),
                  pl.BlockSpec((1, ATT_NEAR, bk, bq), lambda b, h: (h, 0, 0, 0)),
                  pl.BlockSpec((DA_V_DIM, 1), lambda b, h: (0, 0))],
        out_specs=whole,
        out_shape=jax.ShapeDtypeStruct((batch, seq, D_MODEL), BF16),
        compiler_params=_params(2),
        name="attention",
    )(lam, q, k, vt, bias, g_col)


def _ssm_kernel(u_ref, w_ref, c_ref, coef_ref, d_ref, y_ref, h_sc):
    t = pl.program_id(2)
    ns = SSM_HALF_STATES

    @pl.when(t == 0)
    def _():
        h_sc[...] = jnp.zeros(h_sc.shape, F32)

    rows = SSM_ROWS
    groups = rows // SUBLANES
    halves = range(SSM_HALVES)

    def local_states(blk):
        u3 = u_ref[0, blk * rows:(blk + 1) * rows, :].reshape(groups, SUBLANES, LANES)
        row = lax.broadcasted_iota(jnp.int32, u3.shape, 1)
        low = lax.broadcasted_iota(jnp.int32, u3.shape, 2) < SSM_HALF_LANES
        delayed = [u3] + [jnp.where(row >= d, pltpu.roll(u3, d, axis=1), 0.0) for d in range(1, SUBLANES)]
        swapped = [pltpu.roll(ud, SSM_HALF_LANES, axis=2) for ud in delayed]
        out = []
        for h in halves:
            pieces = []
            for d in range(0, SUBLANES, 2):
                pair = (jnp.where(low, delayed[d], swapped[d + 1]) if h == 0
                        else jnp.where(low, swapped[d], delayed[d + 1]))
                pieces.append(pair.reshape(rows, LANES).astype(BF16))
            out.append(jnp.dot(jnp.concatenate(pieces, axis=1), w_ref[0, h], preferred_element_type=F32))
        return out

    def carry_and_project(blk, local, state):
        y = None
        new_state = []
        for h in halves:
            pr = coef_ref[0, h, 0]
            pi = coef_ref[0, h, 1]
            hr, hi = state[h]
            xr = local[h][:, :ns].reshape(groups, SUBLANES, ns)
            xi = local[h][:, ns:].reshape(groups, SUBLANES, ns)
            out_r, out_i = [], []
            for g in range(groups):
                gr = xr[g] + (pr * hr - pi * hi)
                gi = xi[g] + (pr * hi + pi * hr)
                hr = gr[SUBLANES - 1:SUBLANES]
                hi = gi[SUBLANES - 1:SUBLANES]
                out_r.append(gr)
                out_i.append(gi)
            new_state.append((hr, hi))
            x_all = jnp.concatenate([jnp.concatenate(out_r, axis=0), jnp.concatenate(out_i, axis=0)], axis=1)
            part = jnp.dot(x_all.astype(BF16), c_ref[0, h], preferred_element_type=F32)
            y = part if y is None else y + part
        sl = slice(blk * rows, (blk + 1) * rows)
        y_ref[0, sl, :] = y + d_ref[...] * u_ref[0, sl, :]
        return new_state

    n_blk = SSM_T // rows
    state = [(h_sc[h, 0], h_sc[h, 1]) for h in halves]
    pending = local_states(0)
    for blk in range(n_blk):
        upcoming = local_states(blk + 1) if blk + 1 < n_blk else None
        state = carry_and_project(blk, pending, state)
        pending = upcoming
    for h in halves:
        h_sc[h, 0] = state[h][0]
        h_sc[h, 1] = state[h][1]


def _ssm(u, w_blk, c_blk, coef, d_skip, batch, seq):
    nb = SSM_LANE_BLOCKS
    act = pl.BlockSpec((1, SSM_T, LANES), lambda b, j, t: (b, t, j))
    return pl.pallas_call(
        _ssm_kernel,
        grid=(batch, nb, seq // SSM_T),
        in_specs=[act,
                  pl.BlockSpec((1, SSM_HALVES, SUBLANES * SSM_HALF_LANES, 2 * SSM_HALF_STATES),
                               lambda b, j, t: (j, 0, 0, 0)),
                  pl.BlockSpec((1, SSM_HALVES, 2 * SSM_HALF_STATES, LANES), lambda b, j, t: (j, 0, 0, 0)),
                  pl.BlockSpec((1, SSM_HALVES, 2, SUBLANES, SSM_HALF_STATES), lambda b, j, t: (j, 0, 0, 0, 0)),
                  pl.BlockSpec((1, LANES), lambda b, j, t: (0, j))],
        out_specs=act,
        out_shape=jax.ShapeDtypeStruct((batch, seq, D_MODEL), F32),
        scratch_shapes=[pltpu.VMEM((SSM_HALVES, 2, 1, SSM_HALF_STATES), F32)],
        compiler_params=_params(3),
        name="ssm",
    )(u, w_blk, c_blk, coef, d_skip)


def _mem_kv_kernel(mem_ref, g_ref, w_ref, k_ref, v_ref):
    mn = _rmsnorm(mem_ref[0], g_ref[...]).astype(BF16)
    kv = jnp.dot(mn, w_ref[...], preferred_element_type=F32)
    k_ref[0] = kv[:, :D_MODEL].astype(BF16)
    v_ref[0] = kv[:, D_MODEL:].astype(BF16)


def _mem_kv(mem, g, w):
    batch, m, _ = mem.shape
    blk = pl.BlockSpec((1, m, D_MODEL), lambda b: (b, 0, 0))
    shape = jax.ShapeDtypeStruct((batch, m, D_MODEL), BF16)
    return pl.pallas_call(
        _mem_kv_kernel,
        grid=(batch,),
        in_specs=[blk, _const_spec((1, D_MODEL)), _const_spec((D_MODEL, 2 * D_MODEL))],
        out_specs=[blk, blk],
        out_shape=[shape, shape],
        compiler_params=_params(1),
        name="mem_kv",
    )(mem, g, w)


def _gelu_tanh(x):
    return 0.5 * x * (1.0 + jnp.tanh(math.sqrt(2.0 / math.pi) * (x + 0.044715 * (x * x * x))))


def _sigmoid(x):
    return 1.0 / (1.0 + jnp.exp(-x))


def _merge_kernel(x_ref, ya_ref, ys_ref, xq_ref, km_ref, vm_ref, g1_ref, wg_ref, glu_w_ref, glu_b_ref,
                  wa_ref, ws_ref, wx_ref, wo_ref, o_ref):
    d = D_MODEL
    x = x_ref[...]
    h = _rmsnorm(x, g1_ref[...]).astype(BF16)

    mixed = _sigmoid(jnp.dot(h, wg_ref[:, 0:d], preferred_element_type=F32)) * jnp.dot(
        ya_ref[...], wa_ref[...], preferred_element_type=F32)

    z = _gelu_tanh(ys_ref[...])
    zb = z.astype(BF16)
    y_ssm = z * _sigmoid(jnp.dot(zb, glu_w_ref[...], preferred_element_type=F32) + glu_b_ref[...])
    mixed = mixed + _sigmoid(jnp.dot(h, wg_ref[:, d:2 * d], preferred_element_type=F32)) * jnp.dot(
        y_ssm.astype(BF16), ws_ref[...], preferred_element_type=F32)

    heads = []
    for hd in range(XA_HEADS):
        sl = slice(hd * XA_HEAD_DIM, (hd + 1) * XA_HEAD_DIM)
        s = lax.dot_general(xq_ref[:, sl], km_ref[0, :, sl], (((1,), (1,)), ((), ())),
                            preferred_element_type=F32) * (XA_HEAD_DIM ** -0.5)
        p = jnp.exp(s - jnp.max(s, axis=-1, keepdims=True))
        p = p * (1.0 / jnp.sum(p, axis=-1, keepdims=True))
        heads.append(jnp.dot(p.astype(BF16), vm_ref[0, :, sl], preferred_element_type=F32))
    y_x = jnp.concatenate(heads, axis=1).astype(BF16)
    mixed = mixed + _sigmoid(jnp.dot(h, wg_ref[:, 2 * d:3 * d], preferred_element_type=F32)) * jnp.dot(
        y_x, wx_ref[...], preferred_element_type=F32)

    o_ref[...] = x + jnp.dot(mixed.astype(BF16), wo_ref[...], preferred_element_type=F32)


def _merge(x2d, y_attn, y_s, xq, kmem, vmem, g1, wg, glu_w, glu_b, wa, ws, wx, wo, seq):
    n = x2d.shape[0]
    tiles_per_batch = seq // MERGE_TM
    m = kmem.shape[1]
    row = pl.BlockSpec((MERGE_TM, D_MODEL), lambda i: (i, 0))
    mem = pl.BlockSpec((1, m, D_MODEL), lambda i: (i // tiles_per_batch, 0, 0))
    sq = _const_spec((D_MODEL, D_MODEL))
    vec = _const_spec((1, D_MODEL))
    return pl.pallas_call(
        _merge_kernel,
        grid=(n // MERGE_TM,),
        in_specs=[row, row, row, row, mem, mem, vec, _const_spec((D_MODEL, 3 * D_MODEL)), sq, vec,
                  sq, sq, sq, sq],
        out_specs=row,
        out_shape=jax.ShapeDtypeStruct((n, D_MODEL), F32),
        compiler_params=_params(1),
        name="merge",
    )(x2d, y_attn, y_s, xq, kmem, vmem, g1, wg, glu_w, glu_b, wa, ws, wx, wo)


def _ffn_kernel(x_ref, g2_ref, wi_ref, wo_ref, gf_ref, o_ref):
    x = x_ref[...]
    h = _rmsnorm(x, g2_ref[...]).astype(BF16)
    acc = x
    for c in range(FFN_HIDDEN // FFN_CHUNK):
        lo = c * FFN_CHUNK
        gate = jnp.dot(h, wi_ref[:, lo:lo + FFN_CHUNK], preferred_element_type=F32)
        up = jnp.dot(h, wi_ref[:, FFN_HIDDEN + lo:FFN_HIDDEN + lo + FFN_CHUNK], preferred_element_type=F32)
        act = (gate * _sigmoid(gate) * up).astype(BF16)
        acc = acc + jnp.dot(act, wo_ref[lo:lo + FFN_CHUNK, :], preferred_element_type=F32)
    o_ref[...] = _rmsnorm(acc, gf_ref[...])


def _ffn(x2d, g2, wi, wo, gf):
    n = x2d.shape[0]
    row = pl.BlockSpec((FFN_TM, D_MODEL), lambda i: (i, 0))
    vec = _const_spec((1, D_MODEL))
    return pl.pallas_call(
        _ffn_kernel,
        grid=(n // FFN_TM,),
        in_specs=[row, vec, _const_spec((D_MODEL, 2 * FFN_HIDDEN)), _const_spec((FFN_HIDDEN, D_MODEL)), vec],
        out_specs=row,
        out_shape=jax.ShapeDtypeStruct((n, D_MODEL), F32),
        compiler_params=_params(1),
        name="ffn",
    )(x2d, g2, wi, wo, gf)


def kernel(x, mem, norm1_g, w_in, da_lq1, da_lk1, da_lq2, da_lk2, da_subln_g, rel_bias, ssm_a_re, ssm_a_im, ssm_log_dt, ssm_b_re, ssm_b_im, ssm_c_re, ssm_c_im, ssm_d, glu_w, glu_b, mem_norm_g, w_mem_kv, w_br_attn, w_br_ssm, w_br_xattn, w_out, norm2_g, w_ffn_in, w_ffn_out, final_g):
    batch, seq, d = x.shape
    depth = w_in.shape[0]
    assert depth == 1 and d == D_MODEL and seq % PROJ_TM == 0
    layer = 0
    x2d = x.reshape(batch * seq, d)
    row = lambda v: v.reshape(1, -1).astype(F32)

    w = w_in[layer]
    scale = DA_HEAD_DIM ** -0.5 * LOG2E
    w4 = jnp.concatenate([w[:, 0:d] * scale, w[:, d:2 * d], w[:, 3 * d:4 * d], w[:, 4 * d:5 * d]],
                         axis=1).astype(BF16)
    wvt = jnp.transpose(w[:, 2 * d:3 * d]).astype(BF16)
    wg = w[:, 5 * d:8 * d].astype(BF16)

    bias, lam_tile = _prep_attn(rel_bias, row(da_lq1[layer]), row(da_lk1[layer]), row(da_lq2[layer]),
                                row(da_lk2[layer]))
    lam = lam_tile[0:1, 0:1]

    pow_re, pow_im, w_re, w_im = _prep_ssm(ssm_a_re[layer], ssm_a_im[layer], ssm_log_dt[layer],
                                           ssm_b_re[layer], ssm_b_im[layer])
    w_blk, c_blk, coef = _ssm_tables(pow_re, pow_im, w_re, w_im, ssm_c_re[layer], ssm_c_im[layer])

    q, k, u, xq, vt = _proj(x2d, row(norm1_g[layer]), w4, wvt, batch, seq)
    y_attn = _attention(lam, q.reshape(batch, seq, d), k.reshape(batch, seq, d), vt, bias,
                        da_subln_g[layer].reshape(DA_V_DIM, 1).astype(F32), batch, seq)
    y_s = _ssm(u.reshape(batch, seq, d), w_blk, c_blk, coef, row(ssm_d[layer]), batch, seq)
    kmem, vmem = _mem_kv(mem, row(mem_norm_g[layer]), w_mem_kv[layer].astype(BF16))
    x_mid = _merge(x2d, y_attn.reshape(batch * seq, d), y_s.reshape(batch * seq, d), xq, kmem, vmem,
                   row(norm1_g[layer]), wg, glu_w[layer].astype(BF16), row(glu_b[layer]),
                   w_br_attn[layer].astype(BF16), w_br_ssm[layer].astype(BF16), w_br_xattn[layer].astype(BF16),
                   w_out[layer].astype(BF16), seq)
    out = _ffn(x_mid, row(norm2_g[layer]), w_ffn_in[layer].astype(BF16), w_ffn_out[layer].astype(BF16),
               row(final_g))
    return out.reshape(batch, seq, d)
```

```python
import functools
import math

import jax
import jax.numpy as jnp
import numpy as np
from jax import lax
from jax.experimental import pallas as pl
from jax.experimental.pallas import tpu as pltpu

F32 = jnp.float32
BF16 = jnp.bfloat16

D_MODEL = 1024
CHUNK = 64
DA_HEADS = 8
DA_HEAD_DIM = 64
DA_V_DIM = 128
SSM_GROUP = 16
SSM_GROUPS = 64
SSM_STATE = 64
XA_HEADS = 4
XA_HEAD_DIM = 256
REL_BUCKETS = 32
REL_MAX_DIST = 256
FFN_HIDDEN = 2816
RMS_EPS = 1e-6
LAM_INIT = 0.8 - 0.6 * math.exp(-0.3 * 0)

LANES = 128
SUBLANES = 8
VMEM_LIMIT_BYTES = 56 * 1024 * 1024

ATT_BQ = 512
ATT_BK = 512
ATT_NEAR = ATT_BQ // ATT_BK + 1
ATT_V_ROWS = DA_V_DIM + 16
LOG2E = math.log2(math.e)
PROJ_TM = 1024
SSM_T = 4096
SSM_ROWS = 256
SSM_LANE_BLOCKS = D_MODEL // LANES
SSM_HALVES = 2
SSM_HALF_LANES = LANES // SSM_HALVES
SSM_GROUPS_PER_HALF = SSM_HALF_LANES // SSM_GROUP
SSM_HALF_STATES = SSM_GROUPS_PER_HALF * SSM_STATE
MERGE_TM = 512
FFN_TM = 1024
FFN_CHUNK = 256
MASK_VALUE = -1e30
MASK_BUCKET = REL_BUCKETS


def _params(n_axes, flags=None):
    return pltpu.CompilerParams(dimension_semantics=("arbitrary",) * n_axes,
                                vmem_limit_bytes=VMEM_LIMIT_BYTES, flags=flags)


def _const_spec(shape):
    nd = len(shape)
    return pl.BlockSpec(shape, lambda *_: (0,) * nd, pipeline_mode=pl.Buffered(1))


def _rmsnorm(xf, g):
    return xf * lax.rsqrt(jnp.mean(xf * xf, axis=-1, keepdims=True) + RMS_EPS) * g


def _t5_bucket_np(rel):
    half = REL_BUCKETS // 2
    max_exact = half // 2
    ret = np.where(rel > 0, half, 0)
    n = np.abs(rel)
    nf = np.maximum(n, 1).astype(np.float32)
    large = max_exact + (np.log(nf / np.float32(max_exact)) / np.float32(math.log(REL_MAX_DIST / max_exact))
                         * np.float32(half - max_exact)).astype(np.int32)
    large = np.minimum(large, half - 1)
    return (ret + np.where(n < max_exact, n, large)).astype(np.int32)


def _bucket_tiles():
    kk = np.arange(ATT_BK)[:, None]
    qq = np.arange(ATT_BQ)[None, :]
    tiles = []
    for n in range(ATT_NEAR):
        key = kk + (n - 1) * ATT_BK
        tile = _t5_bucket_np(key - qq)
        tiles.append(np.where((key // CHUNK) <= (qq // CHUNK), tile, MASK_BUCKET))
    nearest_far = (-2 * ATT_BK + ATT_BK - 1) - 0
    assert (_t5_bucket_np(np.arange(-8 * ATT_BQ, nearest_far + 1)) == REL_BUCKETS // 2 - 1).all()
    assert (tiles[-1][ATT_BK // 2:, :ATT_BQ // 2] == MASK_BUCKET).all()
    return np.stack(tiles).astype(np.int32)


def _prep_attn_kernel(present, relb_ref, bucket_ref, lq1_ref, lk1_ref, lq2_ref, lk2_ref, bias_ref, lam_ref):
    h = pl.program_id(0)
    for n, buckets in enumerate(present):
        bk = bucket_ref[n]
        acc = jnp.full(bk.shape, MASK_VALUE, F32)
        for b in buckets:
            acc = jnp.where(bk == b, relb_ref[b, h], acc)
        bias_ref[0, n] = (acc - relb_ref[REL_BUCKETS // 2 - 1, h]) * LOG2E
    lam = (jnp.exp(jnp.sum(lq1_ref[...] * lk1_ref[...], axis=-1, keepdims=True))
           - jnp.exp(jnp.sum(lq2_ref[...] * lk2_ref[...], axis=-1, keepdims=True)) + LAM_INIT)
    lam_ref[...] = jnp.broadcast_to(lam, lam_ref.shape)


def _prep_attn(rel_bias, lq1, lk1, lq2, lk2):
    tiles = _bucket_tiles()
    present = tuple(tuple(int(b) for b in np.unique(t) if b != MASK_BUCKET) for t in tiles)
    buckets = jnp.asarray(tiles)
    vec = pl.BlockSpec((1, DA_HEAD_DIM), lambda h: (0, 0))
    return pl.pallas_call(
        functools.partial(_prep_attn_kernel, present),
        grid=(DA_HEADS,),
        in_specs=[pl.BlockSpec(memory_space=pltpu.SMEM),
                  pl.BlockSpec((ATT_NEAR, ATT_BK, ATT_BQ), lambda h: (0, 0, 0)),
                  vec, vec, vec, vec],
        out_specs=[pl.BlockSpec((1, ATT_NEAR, ATT_BK, ATT_BQ), lambda h: (h, 0, 0, 0)),
                   pl.BlockSpec((SUBLANES, LANES), lambda h: (0, 0))],
        out_shape=[jax.ShapeDtypeStruct((DA_HEADS, ATT_NEAR, ATT_BK, ATT_BQ), F32),
                   jax.ShapeDtypeStruct((SUBLANES, LANES), F32)],
        compiler_params=_params(1),
        name="prep_attn",
    )(rel_bias, buckets, lq1, lk1, lq2, lk2)


def _prep_ssm_kernel(are_ref, aim_ref, ldt_ref, bre_ref, bim_ref, pr_ref, pi_ref, wr_ref, wi_ref):
    a_re = are_ref[...]
    a_im = aim_ref[...]
    dt = jnp.exp(ldt_ref[...])
    for n in range(1, SUBLANES + 1):
        mag = jnp.exp(a_re * dt * n)
        ang = a_im * dt * n
        pr_ref[n - 1] = mag * jnp.cos(ang)
        pi_ref[n - 1] = mag * jnp.sin(ang)
    xr = pr_ref[0] - 1.0
    xi = pi_ref[0]
    den = a_re * a_re + a_im * a_im
    cr = ((xr * a_re + xi * a_im) / den)[:, None, :]
    ci = ((xi * a_re - xr * a_im) / den)[:, None, :]
    b_re = bre_ref[...]
    b_im = bim_ref[...]
    bbr = cr * b_re - ci * b_im
    bbi = cr * b_im + ci * b_re
    wr_ref[0] = bbr
    wi_ref[0] = bbi
    for d in range(1, SUBLANES):
        ar = pr_ref[d - 1][:, None, :]
        ai = pi_ref[d - 1][:, None, :]
        wr_ref[d] = ar * bbr - ai * bbi
        wi_ref[d] = ar * bbi + ai * bbr


def _prep_ssm(a_re, a_im, log_dt, b_re, b_im):
    g, p = a_re.shape
    bt_re = jnp.transpose(b_re, (0, 2, 1))
    bt_im = jnp.transpose(b_im, (0, 2, 1))
    pow_shape = jax.ShapeDtypeStruct((SUBLANES, g, p), F32)
    w_shape = jax.ShapeDtypeStruct((SUBLANES, g, SSM_GROUP, p), F32)
    return pl.pallas_call(
        _prep_ssm_kernel,
        out_shape=[pow_shape, pow_shape, w_shape, w_shape],
        name="prep_ssm",
    )(a_re, a_im, log_dt.reshape(g, 1), bt_re, bt_im)


def _ssm_tables(pow_re, pow_im, w_re, w_im, c_re, c_im):
    nb, nh, gh = SSM_LANE_BLOCKS, SSM_HALVES, SSM_GROUPS_PER_HALF
    eye_g = jnp.eye(gh, dtype=bool)
    eye_h = jnp.eye(nh, dtype=bool)

    def w_block(w):
        w = jnp.transpose(w.reshape(SUBLANES, nb, nh, gh, SSM_GROUP, 1, SSM_STATE), (1, 2, 0, 3, 4, 5, 6))
        same_group = eye_g[None, None, None, :, None, :, None]
        return jnp.where(same_group, w, 0.0).reshape(nb, nh, SUBLANES * SSM_HALF_LANES, SSM_HALF_STATES)

    def c_block(cc):
        cc = jnp.transpose(cc.reshape(nb, nh, gh, SSM_GROUP, SSM_STATE), (0, 1, 2, 4, 3))
        cc = cc.reshape(nb, nh, gh, SSM_STATE, 1, 1, SSM_GROUP)
        same = (eye_h[None, :, None, None, :, None, None] & eye_g[None, None, :, None, None, :, None])
        return jnp.where(same, cc, 0.0).reshape(nb, nh, SSM_HALF_STATES, LANES)

    w_blk = jnp.concatenate([w_block(w_re), w_block(w_im)], axis=3).astype(BF16)
    c_blk = jnp.concatenate([c_block(c_re), c_block(-c_im)], axis=2).astype(BF16)

    def lanes(t):
        return jnp.transpose(t.reshape(SUBLANES, nb, nh, SSM_HALF_STATES), (1, 2, 0, 3))

    coef = jnp.stack([lanes(pow_re), lanes(pow_im)], axis=2)
    return w_blk, c_blk, coef


def _proj_kernel(x_ref, g_ref, w_ref, wvt_ref, q_ref, k_ref, u_ref, xq_ref, vt_ref):
    h = _rmsnorm(x_ref[...], g_ref[...]).astype(BF16)
    d = D_MODEL
    q_ref[...] = jnp.dot(h, w_ref[:, 0:d], preferred_element_type=F32).astype(BF16)
    k_ref[...] = jnp.dot(h, w_ref[:, d:2 * d], preferred_element_type=F32).astype(BF16)
    u_ref[...] = jnp.dot(h, w_ref[:, 2 * d:3 * d], preferred_element_type=F32)
    xq_ref[...] = jnp.dot(h, w_ref[:, 3 * d:4 * d], preferred_element_type=F32).astype(BF16)
    vt = lax.dot_general(wvt_ref[...], h, (((1,), (1,)), ((), ())), preferred_element_type=F32).astype(BF16)
    for hd in range(DA_HEADS):
        for jb in range(PROJ_TM // ATT_BK):
            vt_ref[0, hd, jb, 0:DA_V_DIM, :] = vt[hd * DA_V_DIM:(hd + 1) * DA_V_DIM,
                                                   jb * ATT_BK:(jb + 1) * ATT_BK]
            tail_row = lax.broadcasted_iota(jnp.int32, (ATT_V_ROWS - DA_V_DIM, ATT_BK), 0)
            vt_ref[0, hd, jb, DA_V_DIM:ATT_V_ROWS, :] = jnp.where(tail_row == 0, 1.0, 0.0).astype(BF16)


def _proj(x2d, g, w4, wvt, batch, seq):
    n = x2d.shape[0]
    tiles_per_batch = seq // PROJ_TM
    blocks_per_tile = PROJ_TM // ATT_BK
    row = pl.BlockSpec((PROJ_TM, D_MODEL), lambda i: (i, 0))
    act = lambda dt: jax.ShapeDtypeStruct((n, D_MODEL), dt)
    return pl.pallas_call(
        _proj_kernel,
        grid=(n // PROJ_TM,),
        in_specs=[row, _const_spec((1, D_MODEL)), _const_spec((D_MODEL, 4 * D_MODEL)),
                  _const_spec((D_MODEL, D_MODEL))],
        out_specs=[row, row, row, row,
                   pl.BlockSpec((1, DA_HEADS, blocks_per_tile, ATT_V_ROWS, ATT_BK),
                                lambda i: (i // tiles_per_batch, 0, i % tiles_per_batch, 0, 0))],
        out_shape=[act(BF16), act(BF16), act(F32), act(BF16),
                   jax.ShapeDtypeStruct((batch, DA_HEADS, seq // ATT_BK, ATT_V_ROWS, ATT_BK), BF16)],
        compiler_params=_params(1),
        name="proj",
    )(x2d, g, w4, wvt)


def _attn_kernel(lam_ref, q_ref, k_ref, vt_ref, bias_ref, g_ref, o_ref):
    ratio = ATT_BQ // ATT_BK
    lam = lam_ref[0, 0]
    gain = g_ref[...] * (1.0 - LAM_INIT)
    lane = lax.broadcasted_iota(jnp.int32, (ATT_BQ, 2 * DA_HEAD_DIM), 1)

    assert ratio == 1
    half = ATT_BK // 2
    nt = (((1,), (1,)), ((), ()))

    def scores(qs, j, tile, diag):
        kb = k_ref[0, j * ATT_BK:(j + 1) * ATT_BK, :]
        out = []
        for c in range(2):
            if not diag:
                s = lax.dot_general(kb, qs[c], nt, preferred_element_type=F32)
                if tile is not None:
                    s = s + bias_ref[0, tile]
                out.append(((s,), jnp.max(s, axis=0, keepdims=True)))
            else:
                lo = lax.dot_general(kb[0:half], qs[c], nt, preferred_element_type=F32)
                lo = lo + bias_ref[0, tile, 0:half, :]
                m_lo = jnp.max(lo, axis=0, keepdims=True)
                hi = lax.dot_general(kb[half:], qs[c][half:], nt, preferred_element_type=F32)
                hi = hi + bias_ref[0, tile, half:, half:]
                m_hi = jnp.max(hi, axis=0, keepdims=True)
                col_max = jnp.concatenate([m_lo[:, :half], jnp.maximum(m_lo[:, half:], m_hi)], axis=1)
                out.append(((lo, hi), col_max))
        return out

    def softmax(sc, m_prev, diag):
        probs, m_next, alpha = [], [], []
        for c in range(2):
            parts, m_new = sc[c]
            if m_prev is None:
                alpha.append(None)
            else:
                m_new = jnp.maximum(m_prev[c], m_new)
                alpha.append(jnp.exp2(m_prev[c] - m_new))
            m_next.append(m_new)
            if not diag:
                probs.append((jnp.exp2(parts[0] - m_new).astype(BF16),))
            else:
                probs.append((jnp.exp2(parts[0] - m_new).astype(BF16),
                              jnp.exp2(parts[1] - m_new[:, half:]).astype(BF16)))
        return probs, m_next, alpha

    def values(j, probs, alpha, acc, diag):
        vtb = vt_ref[0, 0, j]
        out = []
        for c in range(2):
            if not diag:
                pv = jnp.dot(vtb, probs[c][0], preferred_element_type=F32)
            else:
                lo = jnp.dot(vtb[:, 0:half], probs[c][0], preferred_element_type=F32)
                hi = jnp.dot(vtb[:, half:], probs[c][1], preferred_element_type=F32)
                pv = jnp.concatenate([lo[:, :half], lo[:, half:] + hi], axis=1)
            out.append(pv if acc is None else alpha[c] * acc[c] + pv)
        return out

    def finish(i, acc):
        num = [acc[c][0:DA_V_DIM, :] for c in range(2)]
        den = [acc[c][DA_V_DIM:DA_V_DIM + 1, :] for c in range(2)]
        o = num[0] * (1.0 / den[0]) - lam * (num[1] * (1.0 / den[1]))
        o = o * lax.rsqrt(jnp.mean(o * o, axis=0, keepdims=True) + RMS_EPS)
        o_ref[0, i * ATT_BQ:(i + 1) * ATT_BQ, :] = (o * gain).T.astype(BF16)

    for i in range(q_ref.shape[1] // ATT_BQ):
        qh = q_ref[0, i * ATT_BQ:(i + 1) * ATT_BQ, :]
        zero = jnp.zeros_like(qh)
        qs = (jnp.where(lane < DA_HEAD_DIM, qh, zero), jnp.where(lane >= DA_HEAD_DIM, qh, zero))
        first_near = ratio * i - 1
        last = first_near + ATT_NEAR - 1

        def tile_of(j, first_near=first_near):
            return j - first_near if 0 <= j - first_near < ATT_NEAR else None

        sc = scores(qs, 0, tile_of(0), diag=(last == 0))
        m_run, acc, pending = None, None, None
        for t in range(last + 1):
            if pending is not None:
                acc = values(t - 1, pending[0], pending[1], acc, diag=False)
            probs, m_run, alpha = softmax(sc, m_run, diag=(t == last))
            pending = (probs, alpha)
            if t < last:
                sc = scores(qs, t + 1, tile_of(t + 1), diag=(t + 1 == last))
        acc = values(last, pending[0], pending[1], acc, diag=True)
        finish(i, acc)


def _attention(lam, q, k, vt, bias, g_col, batch, seq):
    bq, bk = ATT_BQ, ATT_BK
    whole = pl.BlockSpec((1, seq, DA_V_DIM), lambda b, h: (b, 0, h))
    return pl.pallas_call(
        _attn_kernel,
        grid=(batch, DA_HEADS),
        in_specs=[pl.BlockSpec(memory_space=pltpu.SMEM),
                  whole,
                  whole,
                  pl.BlockSpec((1, 1, seq // bk, ATT_V_ROWS, bk), lambda b, h: (b, h, 0, 0, 0)),
                  pl.BlockSpec((1, ATT_NEAR, bk, bq), lambda b, h: (h, 0, 0, 0)),
                  pl.BlockSpec((DA_V_DIM, 1), lambda b, h: (0, 0))],
        out_specs=whole,
        out_shape=jax.ShapeDtypeStruct((batch, seq, D_MODEL), BF16),
        compiler_params=_params(2),
        name="attention",
    )(lam, q, k, vt, bias, g_col)


def _ssm_kernel(u_ref, w_ref, c_ref, coef_ref, d_ref, y_ref, h_sc):
    t = pl.program_id(2)
    ns = SSM_HALF_STATES

    @pl.when(t == 0)
    def _():
        h_sc[...] = jnp.zeros(h_sc.shape, F32)

    rows = SSM_ROWS
    groups = rows // SUBLANES
    halves = range(SSM_HALVES)

    def local_states(blk):
        u3 = u_ref[0, blk * rows:(blk + 1) * rows, :].reshape(groups, SUBLANES, LANES)
        row = lax.broadcasted_iota(jnp.int32, u3.shape, 1)
        low = lax.broadcasted_iota(jnp.int32, u3.shape, 2) < SSM_HALF_LANES
        delayed = [u3] + [jnp.where(row >= d, pltpu.roll(u3, d, axis=1), 0.0) for d in range(1, SUBLANES)]
        swapped = [pltpu.roll(ud, SSM_HALF_LANES, axis=2) for ud in delayed]
        out = []
        for h in halves:
            pieces = []
            for d in range(0, SUBLANES, 2):
                pair = (jnp.where(low, delayed[d], swapped[d + 1]) if h == 0
                        else jnp.where(low, swapped[d], delayed[d + 1]))
                pieces.append(pair.reshape(rows, LANES).astype(BF16))
            out.append(jnp.dot(jnp.concatenate(pieces, axis=1), w_ref[0, h], preferred_element_type=F32))
        return out

    def carry_and_project(blk, local, state):
        y = None
        new_state = []
        for h in halves:
            pr = coef_ref[0, h, 0]
            pi = coef_ref[0, h, 1]
            hr, hi = state[h]
            xr = local[h][:, :ns].reshape(groups, SUBLANES, ns)
            xi = local[h][:, ns:].reshape(groups, SUBLANES, ns)
            out_r, out_i = [], []
            for g in range(groups):
                gr = xr[g] + (pr * hr - pi * hi)
                gi = xi[g] + (pr * hi + pi * hr)
                hr = gr[SUBLANES - 1:SUBLANES]
                hi = gi[SUBLANES - 1:SUBLANES]
                out_r.append(gr)
                out_i.append(gi)
            new_state.append((hr, hi))
            x_all = jnp.concatenate([jnp.concatenate(out_r, axis=0), jnp.concatenate(out_i, axis=0)], axis=1)
            part = jnp.dot(x_all.astype(BF16), c_ref[0, h], preferred_element_type=F32)
            y = part if y is None else y + part
        sl = slice(blk * rows, (blk + 1) * rows)
        y_ref[0, sl, :] = y + d_ref[...] * u_ref[0, sl, :]
        return new_state

    n_blk = SSM_T // rows
    state = [(h_sc[h, 0], h_sc[h, 1]) for h in halves]
    pending = local_states(0)
    for blk in range(n_blk):
        upcoming = local_states(blk + 1) if blk + 1 < n_blk else None
        state = carry_and_project(blk, pending, state)
        pending = upcoming
    for h in halves:
        h_sc[h, 0] = state[h][0]
        h_sc[h, 1] = state[h][1]


def _ssm(u, w_blk, c_blk, coef, d_skip, batch, seq):
    nb = SSM_LANE_BLOCKS
    act = pl.BlockSpec((1, SSM_T, LANES), lambda b, j, t: (b, t, j))
    return pl.pallas_call(
        _ssm_kernel,
        grid=(batch, nb, seq // SSM_T),
        in_specs=[act,
                  pl.BlockSpec((1, SSM_HALVES, SUBLANES * SSM_HALF_LANES, 2 * SSM_HALF_STATES),
                               lambda b, j, t: (j, 0, 0, 0)),
                  pl.BlockSpec((1, SSM_HALVES, 2 * SSM_HALF_STATES, LANES), lambda b, j, t: (j, 0, 0, 0)),
                  pl.BlockSpec((1, SSM_HALVES, 2, SUBLANES, SSM_HALF_STATES), lambda b, j, t: (j, 0, 0, 0, 0)),
                  pl.BlockSpec((1, LANES), lambda b, j, t: (0, j))],
        out_specs=act,
        out_shape=jax.ShapeDtypeStruct((batch, seq, D_MODEL), F32),
        scratch_shapes=[pltpu.VMEM((SSM_HALVES, 2, 1, SSM_HALF_STATES), F32)],
        compiler_params=_params(3),
        name="ssm",
    )(u, w_blk, c_blk, coef, d_skip)


def _mem_kv_kernel(mem_ref, g_ref, w_ref, k_ref, v_ref):
    mn = _rmsnorm(mem_ref[0], g_ref[...]).astype(BF16)
    kv = jnp.dot(mn, w_ref[...], preferred_element_type=F32)
    k_ref[0] = kv[:, :D_MODEL].astype(BF16)
    v_ref[0] = kv[:, D_MODEL:].astype(BF16)


def _mem_kv(mem, g, w):
    batch, m, _ = mem.shape
    blk = pl.BlockSpec((1, m, D_MODEL), lambda b: (b, 0, 0))
    shape = jax.ShapeDtypeStruct((batch, m, D_MODEL), BF16)
    return pl.pallas_call(
        _mem_kv_kernel,
        grid=(batch,),
        in_specs=[blk, _const_spec((1, D_MODEL)), _const_spec((D_MODEL, 2 * D_MODEL))],
        out_specs=[blk, blk],
        out_shape=[shape, shape],
        compiler_params=_params(1),
        name="mem_kv",
    )(mem, g, w)


def _gelu_tanh(x):
    return 0.5 * x * (1.0 + jnp.tanh(math.sqrt(2.0 / math.pi) * (x + 0.044715 * (x * x * x))))


def _sigmoid(x):
    return 1.0 / (1.0 + jnp.exp(-x))


def _merge_kernel(x_ref, ya_ref, ys_ref, xq_ref, km_ref, vm_ref, g1_ref, wg_ref, glu_w_ref, glu_b_ref,
                  wa_ref, ws_ref, wx_ref, wo_ref, o_ref):
    d = D_MODEL
    x = x_ref[...]
    h = _rmsnorm(x, g1_ref[...]).astype(BF16)

    mixed = _sigmoid(jnp.dot(h, wg_ref[:, 0:d], preferred_element_type=F32)) * jnp.dot(
        ya_ref[...], wa_ref[...], preferred_element_type=F32)

    z = _gelu_tanh(ys_ref[...])
    zb = z.astype(BF16)
    y_ssm = z * _sigmoid(jnp.dot(zb, glu_w_ref[...], preferred_element_type=F32) + glu_b_ref[...])
    mixed = mixed + _sigmoid(jnp.dot(h, wg_ref[:, d:2 * d], preferred_element_type=F32)) * jnp.dot(
        y_ssm.astype(BF16), ws_ref[...], preferred_element_type=F32)

    heads = []
    for hd in range(XA_HEADS):
        sl = slice(hd * XA_HEAD_DIM, (hd + 1) * XA_HEAD_DIM)
        s = lax.dot_general(xq_ref[:, sl], km_ref[0, :, sl], (((1,), (1,)), ((), ())),
                            preferred_element_type=F32) * (XA_HEAD_DIM ** -0.5)
        p = jnp.exp(s - jnp.max(s, axis=-1, keepdims=True))
        p = p * (1.0 / jnp.sum(p, axis=-1, keepdims=True))
        heads.append(jnp.dot(p.astype(BF16), vm_ref[0, :, sl], preferred_element_type=F32))
    y_x = jnp.concatenate(heads, axis=1).astype(BF16)
    mixed = mixed + _sigmoid(jnp.dot(h, wg_ref[:, 2 * d:3 * d], preferred_element_type=F32)) * jnp.dot(
        y_x, wx_ref[...], preferred_element_type=F32)

    o_ref[...] = x + jnp.dot(mixed.astype(BF16), wo_ref[...], preferred_element_type=F32)


def _merge(x2d, y_attn, y_s, xq, kmem, vmem, g1, wg, glu_w, glu_b, wa, ws, wx, wo, seq):
    n = x2d.shape[0]
    tiles_per_batch = seq // MERGE_TM
    m = kmem.shape[1]
    row = pl.BlockSpec((MERGE_TM, D_MODEL), lambda i: (i, 0))
    mem = pl.BlockSpec((1, m, D_MODEL), lambda i: (i // tiles_per_batch, 0, 0))
    sq = _const_spec((D_MODEL, D_MODEL))
    vec = _const_spec((1, D_MODEL))
    return pl.pallas_call(
        _merge_kernel,
        grid=(n // MERGE_TM,),
        in_specs=[row, row, row, row, mem, mem, vec, _const_spec((D_MODEL, 3 * D_MODEL)), sq, vec,
                  sq, sq, sq, sq],
        out_specs=row,
        out_shape=jax.ShapeDtypeStruct((n, D_MODEL), F32),
        compiler_params=_params(1),
        name="merge",
    )(x2d, y_attn, y_s, xq, kmem, vmem, g1, wg, glu_w, glu_b, wa, ws, wx, wo)


def _ffn_kernel(x_ref, g2_ref, wi_ref, wo_ref, gf_ref, o_ref):
    x = x_ref[...]
    h = _rmsnorm(x, g2_ref[...]).astype(BF16)
    acc = x
    for c in range(FFN_HIDDEN // FFN_CHUNK):
        lo = c * FFN_CHUNK
        gate = jnp.dot(h, wi_ref[:, lo:lo + FFN_CHUNK], preferred_element_type=F32)
        up = jnp.dot(h, wi_ref[:, FFN_HIDDEN + lo:FFN_HIDDEN + lo + FFN_CHUNK], preferred_element_type=F32)
        act = (gate * _sigmoid(gate) * up).astype(BF16)
        acc = acc + jnp.dot(act, wo_ref[lo:lo + FFN_CHUNK, :], preferred_element_type=F32)
    o_ref[...] = _rmsnorm(acc, gf_ref[...])


def _ffn(x2d, g2, wi, wo, gf):
    assert FFN_HIDDEN % FFN_CHUNK == 0
    n = x2d.shape[0]
    row = pl.BlockSpec((FFN_TM, D_MODEL), lambda i: (i, 0))
    vec = _const_spec((1, D_MODEL))
    return pl.pallas_call(
        _ffn_kernel,
        grid=(n // FFN_TM,),
        in_specs=[row, vec, _const_spec((D_MODEL, 2 * FFN_HIDDEN)), _const_spec((FFN_HIDDEN, D_MODEL)), vec],
        out_specs=row,
        out_shape=jax.ShapeDtypeStruct((n, D_MODEL), F32),
        compiler_params=_params(1),
        name="ffn",
    )(x2d, g2, wi, wo, gf)


def kernel(x, mem, norm1_g, w_in, da_lq1, da_lk1, da_lq2, da_lk2, da_subln_g, rel_bias, ssm_a_re, ssm_a_im, ssm_log_dt, ssm_b_re, ssm_b_im, ssm_c_re, ssm_c_im, ssm_d, glu_w, glu_b, mem_norm_g, w_mem_kv, w_br_attn, w_br_ssm, w_br_xattn, w_out, norm2_g, w_ffn_in, w_ffn_out, final_g):
    batch, seq, d = x.shape
    depth = w_in.shape[0]
    assert depth == 1 and d == D_MODEL and seq % PROJ_TM == 0
    layer = 0
    x2d = x.reshape(batch * seq, d)
    row = lambda v: v.reshape(1, -1).astype(F32)

    w = w_in[layer]
    scale = DA_HEAD_DIM ** -0.5 * LOG2E
    w4 = jnp.concatenate([w[:, 0:d] * scale, w[:, d:2 * d], w[:, 3 * d:4 * d], w[:, 4 * d:5 * d]],
                         axis=1).astype(BF16)
    wvt = jnp.transpose(w[:, 2 * d:3 * d]).astype(BF16)
    wg = w[:, 5 * d:8 * d].astype(BF16)

    bias, lam_tile = _prep_attn(rel_bias, row(da_lq1[layer]), row(da_lk1[layer]), row(da_lq2[layer]),
                                row(da_lk2[layer]))
    lam = lam_tile[0:1, 0:1]

    pow_re, pow_im, w_re, w_im = _prep_ssm(ssm_a_re[layer], ssm_a_im[layer], ssm_log_dt[layer],
                                           ssm_b_re[layer], ssm_b_im[layer])
    w_blk, c_blk, coef = _ssm_tables(pow_re, pow_im, w_re, w_im, ssm_c_re[layer], ssm_c_im[layer])

    q, k, u, xq, vt = _proj(x2d, row(norm1_g[layer]), w4, wvt, batch, seq)
    y_attn = _attention(lam, q.reshape(batch, seq, d), k.reshape(batch, seq, d), vt, bias,
                        da_subln_g[layer].reshape(DA_V_DIM, 1).astype(F32), batch, seq)
    y_s = _ssm(u.reshape(batch, seq, d), w_blk, c_blk, coef, row(ssm_d[layer]), batch, seq)
    kmem, vmem = _mem_kv(mem, row(mem_norm_g[layer]), w_mem_kv[layer].astype(BF16))
    x_mid = _merge(x2d, y_attn.reshape(batch * seq, d), y_s.reshape(batch * seq, d), xq, kmem, vmem,
                   row(norm1_g[layer]), wg, glu_w[layer].astype(BF16), row(glu_b[layer]),
                   w_br_attn[layer].astype(BF16), w_br_ssm[layer].astype(BF16), w_br_xattn[layer].astype(BF16),
                   w_out[layer].astype(BF16), seq)
    out = _ffn(x_mid, row(norm2_g[layer]), w_ffn_in[layer].astype(BF16), w_ffn_out[layer].astype(BF16),
               row(final_g))
    return out.reshape(batch, seq, d)
```

```python
import functools
import math

import jax
import jax.numpy as jnp
import numpy as np
from jax import lax
from jax.experimental import pallas as pl
from jax.experimental.pallas import tpu as pltpu

F32 = jnp.float32
BF16 = jnp.bfloat16

D_MODEL = 1024
CHUNK = 64
DA_HEADS = 8
DA_HEAD_DIM = 64
DA_V_DIM = 128
SSM_GROUP = 16
SSM_GROUPS = 64
SSM_STATE = 64
XA_HEADS = 4
XA_HEAD_DIM = 256
REL_BUCKETS = 32
REL_MAX_DIST = 256
FFN_HIDDEN = 2816
RMS_EPS = 1e-6
LAM_INIT = 0.8 - 0.6 * math.exp(-0.3 * 0)

LANES = 128
SUBLANES = 8
VMEM_LIMIT_BYTES = 56 * 1024 * 1024

ATT_BQ = 512
ATT_BK = 512
ATT_NEAR = ATT_BQ // ATT_BK + 1
ATT_V_ROWS = DA_V_DIM + 16
LOG2E = math.log2(math.e)
PROJ_TM = 1024
SSM_T = 4096
SSM_ROWS = 256
SSM_LANE_BLOCKS = D_MODEL // LANES
SSM_HALVES = 2
SSM_HALF_LANES = LANES // SSM_HALVES
SSM_GROUPS_PER_HALF = SSM_HALF_LANES // SSM_GROUP
SSM_HALF_STATES = SSM_GROUPS_PER_HALF * SSM_STATE
MERGE_TM = 512
FFN_TM = 1024
FFN_CHUNK = 256
MASK_VALUE = -1e30
MASK_BUCKET = REL_BUCKETS


def _params(n_axes, flags=None):
    return pltpu.CompilerParams(dimension_semantics=("arbitrary",) * n_axes,
                                vmem_limit_bytes=VMEM_LIMIT_BYTES, flags=flags)


def _const_spec(shape):
    nd = len(shape)
    return pl.BlockSpec(shape, lambda *_: (0,) * nd, pipeline_mode=pl.Buffered(1))


def _rmsnorm(xf, g):
    return xf * lax.rsqrt(jnp.mean(xf * xf, axis=-1, keepdims=True) + RMS_EPS) * g


def _t5_bucket_np(rel):
    half = REL_BUCKETS // 2
    max_exact = half // 2
    ret = np.where(rel > 0, half, 0)
    n = np.abs(rel)
    nf = np.maximum(n, 1).astype(np.float32)
    large = max_exact + (np.log(nf / np.float32(max_exact)) / np.float32(math.log(REL_MAX_DIST / max_exact))
                         * np.float32(half - max_exact)).astype(np.int32)
    large = np.minimum(large, half - 1)
    return (ret + np.where(n < max_exact, n, large)).astype(np.int32)


def _bucket_tiles():
    kk = np.arange(ATT_BK)[:, None]
    qq = np.arange(ATT_BQ)[None, :]
    tiles = []
    for n in range(ATT_NEAR):
        key = kk + (n - 1) * ATT_BK
        tile = _t5_bucket_np(key - qq)
        tiles.append(np.where((key // CHUNK) <= (qq // CHUNK), tile, MASK_BUCKET))
    nearest_far = (-2 * ATT_BK + ATT_BK - 1) - 0
    assert (_t5_bucket_np(np.arange(-8 * ATT_BQ, nearest_far + 1)) == REL_BUCKETS // 2 - 1).all()
    assert (tiles[-1][ATT_BK // 2:, :ATT_BQ // 2] == MASK_BUCKET).all()
    return np.stack(tiles).astype(np.int32)


def _prep_attn_kernel(present, relb_ref, bucket_ref, lq1_ref, lk1_ref, lq2_ref, lk2_ref, bias_ref, lam_ref):
    h = pl.program_id(0)
    for n, buckets in enumerate(present):
        bk = bucket_ref[n]
        acc = jnp.full(bk.shape, MASK_VALUE, F32)
        for b in buckets:
            acc = jnp.where(bk == b, relb_ref[b, h], acc)
        bias_ref[0, n] = (acc - relb_ref[REL_BUCKETS // 2 - 1, h]) * LOG2E
    lam = (jnp.exp(jnp.sum(lq1_ref[...] * lk1_ref[...], axis=-1, keepdims=True))
           - jnp.exp(jnp.sum(lq2_ref[...] * lk2_ref[...], axis=-1, keepdims=True)) + LAM_INIT)
    lam_ref[...] = jnp.broadcast_to(lam, lam_ref.shape)


def _prep_attn(rel_bias, lq1, lk1, lq2, lk2):
    tiles = _bucket_tiles()
    present = tuple(tuple(int(b) for b in np.unique(t) if b != MASK_BUCKET) for t in tiles)
    buckets = jnp.asarray(tiles)
    vec = pl.BlockSpec((1, DA_HEAD_DIM), lambda h: (0, 0))
    return pl.pallas_call(
        functools.partial(_prep_attn_kernel, present),
        grid=(DA_HEADS,),
        in_specs=[pl.BlockSpec(memory_space=pltpu.SMEM),
                  pl.BlockSpec((ATT_NEAR, ATT_BK, ATT_BQ), lambda h: (0, 0, 0)),
                  vec, vec, vec, vec],
        out_specs=[pl.BlockSpec((1, ATT_NEAR, ATT_BK, ATT_BQ), lambda h: (h, 0, 0, 0)),
                   pl.BlockSpec((SUBLANES, LANES), lambda h: (0, 0))],
        out_shape=[jax.ShapeDtypeStruct((DA_HEADS, ATT_NEAR, ATT_BK, ATT_BQ), F32),
                   jax.ShapeDtypeStruct((SUBLANES, LANES), F32)],
        compiler_params=_params(1),
        name="prep_attn",
    )(rel_bias, buckets, lq1, lk1, lq2, lk2)


def _prep_ssm_kernel(are_ref, aim_ref, ldt_ref, bre_ref, bim_ref, pr_ref, pi_ref, wr_ref, wi_ref):
    a_re = are_ref[...]
    a_im = aim_ref[...]
    dt = jnp.exp(ldt_ref[...])
    for n in range(1, SUBLANES + 1):
        mag = jnp.exp(a_re * dt * n)
        ang = a_im * dt * n
        pr_ref[n - 1] = mag * jnp.cos(ang)
        pi_ref[n - 1] = mag * jnp.sin(ang)
    xr = pr_ref[0] - 1.0
    xi = pi_ref[0]
    den = a_re * a_re + a_im * a_im
    cr = ((xr * a_re + xi * a_im) / den)[:, None, :]
    ci = ((xi * a_re - xr * a_im) / den)[:, None, :]
    b_re = bre_ref[...]
    b_im = bim_ref[...]
    bbr = cr * b_re - ci * b_im
    bbi = cr * b_im + ci * b_re
    wr_ref[0] = bbr
    wi_ref[0] = bbi
    for d in range(1, SUBLANES):
        ar = pr_ref[d - 1][:, None, :]
        ai = pi_ref[d - 1][:, None, :]
        wr_ref[d] = ar * bbr - ai * bbi
        wi_ref[d] = ar * bbi + ai * bbr


def _prep_ssm(a_re, a_im, log_dt, b_re, b_im):
    g, p = a_re.shape
    bt_re = jnp.transpose(b_re, (0, 2, 1))
    bt_im = jnp.transpose(b_im, (0, 2, 1))
    pow_shape = jax.ShapeDtypeStruct((SUBLANES, g, p), F32)
    w_shape = jax.ShapeDtypeStruct((SUBLANES, g, SSM_GROUP, p), F32)
    return pl.pallas_call(
        _prep_ssm_kernel,
        out_shape=[pow_shape, pow_shape, w_shape, w_shape],
        name="prep_ssm",
    )(a_re, a_im, log_dt.reshape(g, 1), bt_re, bt_im)


def _ssm_tables(pow_re, pow_im, w_re, w_im, c_re, c_im):
    nb, nh, gh = SSM_LANE_BLOCKS, SSM_HALVES, SSM_GROUPS_PER_HALF
    eye_g = jnp.eye(gh, dtype=bool)
    eye_h = jnp.eye(nh, dtype=bool)

    def w_block(w):
        w = w.astype(BF16).reshape(SUBLANES, nb, nh, gh, SSM_GROUP, 1, SSM_STATE)
        w = jnp.transpose(w, (1, 2, 0, 3, 4, 5, 6))
        same_group = eye_g[None, None, None, :, None, :, None]
        zero = jnp.zeros((), BF16)
        return jnp.where(same_group, w, zero).reshape(nb, nh, SUBLANES * SSM_HALF_LANES, SSM_HALF_STATES)

    def c_block(cc):
        cc = jnp.transpose(cc.astype(BF16).reshape(nb, nh, gh, SSM_GROUP, SSM_STATE), (0, 1, 2, 4, 3))
        cc = cc.reshape(nb, nh, gh, SSM_STATE, 1, 1, SSM_GROUP)
        same = (eye_h[None, :, None, None, :, None, None] & eye_g[None, None, :, None, None, :, None])
        return jnp.where(same, cc, jnp.zeros((), BF16)).reshape(nb, nh, SSM_HALF_STATES, LANES)

    w_blk = jnp.concatenate([w_block(w_re), w_block(w_im)], axis=3).astype(BF16)
    c_blk = jnp.concatenate([c_block(c_re), c_block(-c_im)], axis=2).astype(BF16)

    def lanes(t):
        return jnp.transpose(t.reshape(SUBLANES, nb, nh, SSM_HALF_STATES), (1, 2, 0, 3))

    coef = jnp.stack([lanes(pow_re), lanes(pow_im)], axis=2)
    return w_blk, c_blk, coef


def _proj_kernel(x_ref, g_ref, w_ref, wvt_ref, q_ref, k_ref, u_ref, xq_ref, vt_ref):
    h = _rmsnorm(x_ref[...], g_ref[...]).astype(BF16)
    d = D_MODEL
    q_ref[...] = jnp.dot(h, w_ref[:, 0:d], preferred_element_type=F32).astype(BF16)
    k_ref[...] = jnp.dot(h, w_ref[:, d:2 * d], preferred_element_type=F32).astype(BF16)
    u_ref[...] = jnp.dot(h, w_ref[:, 2 * d:3 * d], preferred_element_type=F32)
    xq_ref[...] = jnp.dot(h, w_ref[:, 3 * d:4 * d], preferred_element_type=F32).astype(BF16)
    vt = lax.dot_general(wvt_ref[...], h, (((1,), (1,)), ((), ())), preferred_element_type=F32).astype(BF16)
    for hd in range(DA_HEADS):
        for jb in range(PROJ_TM // ATT_BK):
            vt_ref[0, hd, jb, 0:DA_V_DIM, :] = vt[hd * DA_V_DIM:(hd + 1) * DA_V_DIM,
                                                   jb * ATT_BK:(jb + 1) * ATT_BK]
            tail_row = lax.broadcasted_iota(jnp.int32, (ATT_V_ROWS - DA_V_DIM, ATT_BK), 0)
            vt_ref[0, hd, jb, DA_V_DIM:ATT_V_ROWS, :] = jnp.where(tail_row == 0, 1.0, 0.0).astype(BF16)


def _proj(x2d, g, w4, wvt, batch, seq):
    n = x2d.shape[0]
    tiles_per_batch = seq // PROJ_TM
    blocks_per_tile = PROJ_TM // ATT_BK
    row = pl.BlockSpec((PROJ_TM, D_MODEL), lambda i: (i, 0))
    act = lambda dt: jax.ShapeDtypeStruct((n, D_MODEL), dt)
    return pl.pallas_call(
        _proj_kernel,
        grid=(n // PROJ_TM,),
        in_specs=[row, _const_spec((1, D_MODEL)), _const_spec((D_MODEL, 4 * D_MODEL)),
                  _const_spec((D_MODEL, D_MODEL))],
        out_specs=[row, row, row, row,
                   pl.BlockSpec((1, DA_HEADS, blocks_per_tile, ATT_V_ROWS, ATT_BK),
                                lambda i: (i // tiles_per_batch, 0, i % tiles_per_batch, 0, 0))],
        out_shape=[act(BF16), act(BF16), act(F32), act(BF16),
                   jax.ShapeDtypeStruct((batch, DA_HEADS, seq // ATT_BK, ATT_V_ROWS, ATT_BK), BF16)],
        compiler_params=_params(1),
        name="proj",
    )(x2d, g, w4, wvt)


def _attn_kernel(lam_ref, q_ref, k_ref, vt_ref, bias_ref, g_ref, o_ref):
    ratio = ATT_BQ // ATT_BK
    lam = lam_ref[0, 0]
    gain = g_ref[...] * (1.0 - LAM_INIT)
    lane = lax.broadcasted_iota(jnp.int32, (ATT_BQ, 2 * DA_HEAD_DIM), 1)

    assert ratio == 1
    half = ATT_BK // 2
    nt = (((1,), (1,)), ((), ()))

    def scores(qs, j, tile, diag):
        kb = k_ref[0, j * ATT_BK:(j + 1) * ATT_BK, :]
        out = []
        for c in range(2):
            if not diag:
                s = lax.dot_general(kb, qs[c], nt, preferred_element_type=F32)
                if tile is not None:
                    s = s + bias_ref[0, tile]
                out.append(((s,), jnp.max(s, axis=0, keepdims=True)))
            else:
                lo = lax.dot_general(kb[0:half], qs[c], nt, preferred_element_type=F32)
                lo = lo + bias_ref[0, tile, 0:half, :]
                m_lo = jnp.max(lo, axis=0, keepdims=True)
                hi = lax.dot_general(kb[half:], qs[c][half:], nt, preferred_element_type=F32)
                hi = hi + bias_ref[0, tile, half:, half:]
                m_hi = jnp.max(hi, axis=0, keepdims=True)
                col_max = jnp.concatenate([m_lo[:, :half], jnp.maximum(m_lo[:, half:], m_hi)], axis=1)
                out.append(((lo, hi), col_max))
        return out

    def softmax(sc, m_prev, diag):
        probs, m_next, alpha = [], [], []
        for c in range(2):
            parts, m_new = sc[c]
            if m_prev is None:
                alpha.append(None)
            else:
                m_new = jnp.maximum(m_prev[c], m_new)
                alpha.append(jnp.exp2(m_prev[c] - m_new))
            m_next.append(m_new)
            if not diag:
                probs.append((jnp.exp2(parts[0] - m_new).astype(BF16),))
            else:
                probs.append((jnp.exp2(parts[0] - m_new).astype(BF16),
                              jnp.exp2(parts[1] - m_new[:, half:]).astype(BF16)))
        return probs, m_next, alpha

    def values(j, probs, alpha, acc, diag):
        vtb = vt_ref[0, 0, j]
        out = []
        for c in range(2):
            if not diag:
                pv = jnp.dot(vtb, probs[c][0], preferred_element_type=F32)
            else:
                lo = jnp.dot(vtb[:, 0:half], probs[c][0], preferred_element_type=F32)
                hi = jnp.dot(vtb[:, half:], probs[c][1], preferred_element_type=F32)
                pv = jnp.concatenate([lo[:, :half], lo[:, half:] + hi], axis=1)
            out.append(pv if acc is None else alpha[c] * acc[c] + pv)
        return out

    def finish(i, acc):
        num = [acc[c][0:DA_V_DIM, :] for c in range(2)]
        den = [acc[c][DA_V_DIM:DA_V_DIM + 1, :] for c in range(2)]
        o = num[0] * (1.0 / den[0]) - lam * (num[1] * (1.0 / den[1]))
        o = o * lax.rsqrt(jnp.mean(o * o, axis=0, keepdims=True) + RMS_EPS)
        o_ref[0, i * ATT_BQ:(i + 1) * ATT_BQ, :] = (o * gain).T.astype(BF16)

    for i in range(q_ref.shape[1] // ATT_BQ):
        qh = q_ref[0, i * ATT_BQ:(i + 1) * ATT_BQ, :]
        zero = jnp.zeros_like(qh)
        qs = (jnp.where(lane < DA_HEAD_DIM, qh, zero), jnp.where(lane >= DA_HEAD_DIM, qh, zero))
        first_near = ratio * i - 1
        last = first_near + ATT_NEAR - 1

        def tile_of(j, first_near=first_near):
            return j - first_near if 0 <= j - first_near < ATT_NEAR else None

        sc = scores(qs, 0, tile_of(0), diag=(last == 0))
        m_run, acc, pending = None, None, None
        for t in range(last + 1):
            if pending is not None:
                acc = values(t - 1, pending[0], pending[1], acc, diag=False)
            probs, m_run, alpha = softmax(sc, m_run, diag=(t == last))
            pending = (probs, alpha)
            if t < last:
                sc = scores(qs, t + 1, tile_of(t + 1), diag=(t + 1 == last))
        acc = values(last, pending[0], pending[1], acc, diag=True)
        finish(i, acc)


def _attention(lam, q, k, vt, bias, g_col, batch, seq):
    bq, bk = ATT_BQ, ATT_BK
    whole = pl.BlockSpec((1, seq, DA_V_DIM), lambda b, h: (b, 0, h))
    return pl.pallas_call(
        _attn_kernel,
        grid=(batch, DA_HEADS),
        in_specs=[pl.BlockSpec(memory_space=pltpu.SMEM),
                  whole,
                  whole,
                  pl.BlockSpec((1, 1, seq // bk, ATT_V_ROWS, bk), lambda b, h: (b, h, 0, 0, 0)),
                  pl.BlockSpec((1, ATT_NEAR, bk, bq), lambda b, h: (h, 0, 0, 0)),
                  pl.BlockSpec((DA_V_DIM, 1), lambda b, h: (0, 0))],
        out_specs=whole,
        out_shape=jax.ShapeDtypeStruct((batch, seq, D_MODEL), BF16),
        compiler_params=_params(2),
        name="attention",
    )(lam, q, k, vt, bias, g_col)


def _ssm_kernel(u_ref, w_ref, c_ref, coef_ref, d_ref, y_ref, h_sc):
    t = pl.program_id(2)
    ns = SSM_HALF_STATES

    @pl.when(t == 0)
    def _():
        h_sc[...] = jnp.zeros(h_sc.shape, F32)

    rows = SSM_ROWS
    groups = rows // SUBLANES
    halves = range(SSM_HALVES)

    def local_states(blk):
        u3 = u_ref[0, blk * rows:(blk + 1) * rows, :].reshape(groups, SUBLANES, LANES)
        row = lax.broadcasted_iota(jnp.int32, u3.shape, 1)
        low = lax.broadcasted_iota(jnp.int32, u3.shape, 2) < SSM_HALF_LANES
        delayed = [u3] + [jnp.where(row >= d, pltpu.roll(u3, d, axis=1), 0.0) for d in range(1, SUBLANES)]
        swapped = [pltpu.roll(ud, SSM_HALF_LANES, axis=2) for ud in delayed]
        out = []
        for h in halves:
            pieces = []
            for d in range(0, SUBLANES, 2):
                pair = (jnp.where(low, delayed[d], swapped[d + 1]) if h == 0
                        else jnp.where(low, swapped[d], delayed[d + 1]))
                pieces.append(pair.reshape(rows, LANES).astype(BF16))
            out.append(jnp.dot(jnp.concatenate(pieces, axis=1), w_ref[0, h], preferred_element_type=F32))
        return out

    def carry_and_project(blk, local, state):
        y = None
        new_state = []
        for h in halves:
            pr = coef_ref[0, h, 0]
            pi = coef_ref[0, h, 1]
            hr, hi = state[h]
            xr = local[h][:, :ns].reshape(groups, SUBLANES, ns)
            xi = local[h][:, ns:].reshape(groups, SUBLANES, ns)
            out_r, out_i = [], []
            for g in range(groups):
                gr = xr[g] + (pr * hr - pi * hi)
                gi = xi[g] + (pr * hi + pi * hr)
                hr = gr[SUBLANES - 1:SUBLANES]
                hi = gi[SUBLANES - 1:SUBLANES]
                out_r.append(gr)
                out_i.append(gi)
            new_state.append((hr, hi))
            x_all = jnp.concatenate([jnp.concatenate(out_r, axis=0), jnp.concatenate(out_i, axis=0)], axis=1)
            part = jnp.dot(x_all.astype(BF16), c_ref[0, h], preferred_element_type=F32)
            y = part if y is None else y + part
        sl = slice(blk * rows, (blk + 1) * rows)
        y_ref[0, sl, :] = y + d_ref[...] * u_ref[0, sl, :]
        return new_state

    n_blk = SSM_T // rows
    state = [(h_sc[h, 0], h_sc[h, 1]) for h in halves]
    pending = local_states(0)
    for blk in range(n_blk):
        upcoming = local_states(blk + 1) if blk + 1 < n_blk else None
        state = carry_and_project(blk, pending, state)
        pending = upcoming
    for h in halves:
        h_sc[h, 0] = state[h][0]
        h_sc[h, 1] = state[h][1]


def _ssm(u, w_blk, c_blk, coef, d_skip, batch, seq):
    nb = SSM_LANE_BLOCKS
    act = pl.BlockSpec((1, SSM_T, LANES), lambda b, j, t: (b, t, j))
    return pl.pallas_call(
        _ssm_kernel,
        grid=(batch, nb, seq // SSM_T),
        in_specs=[act,
                  pl.BlockSpec((1, SSM_HALVES, SUBLANES * SSM_HALF_LANES, 2 * SSM_HALF_STATES),
                               lambda b, j, t: (j, 0, 0, 0)),
                  pl.BlockSpec((1, SSM_HALVES, 2 * SSM_HALF_STATES, LANES), lambda b, j, t: (j, 0, 0, 0)),
                  pl.BlockSpec((1, SSM_HALVES, 2, SUBLANES, SSM_HALF_STATES), lambda b, j, t: (j, 0, 0, 0, 0)),
                  pl.BlockSpec((1, LANES), lambda b, j, t: (0, j))],
        out_specs=act,
        out_shape=jax.ShapeDtypeStruct((batch, seq, D_MODEL), F32),
        scratch_shapes=[pltpu.VMEM((SSM_HALVES, 2, 1, SSM_HALF_STATES), F32)],
        compiler_params=_params(3),
        name="ssm",
    )(u, w_blk, c_blk, coef, d_skip)


def _mem_kv_kernel(mem_ref, g_ref, w_ref, k_ref, v_ref):
    mn = _rmsnorm(mem_ref[0], g_ref[...]).astype(BF16)
    kv = jnp.dot(mn, w_ref[...], preferred_element_type=F32)
    k_ref[0] = kv[:, :D_MODEL].astype(BF16)
    v_ref[0] = kv[:, D_MODEL:].astype(BF16)


def _mem_kv(mem, g, w):
    batch, m, _ = mem.shape
    blk = pl.BlockSpec((1, m, D_MODEL), lambda b: (b, 0, 0))
    shape = jax.ShapeDtypeStruct((batch, m, D_MODEL), BF16)
    return pl.pallas_call(
        _mem_kv_kernel,
        grid=(batch,),
        in_specs=[blk, _const_spec((1, D_MODEL)), _const_spec((D_MODEL, 2 * D_MODEL))],
        out_specs=[blk, blk],
        out_shape=[shape, shape],
        compiler_params=_params(1),
        name="mem_kv",
    )(mem, g, w)


def _gelu_tanh(x):
    return 0.5 * x * (1.0 + jnp.tanh(math.sqrt(2.0 / math.pi) * (x + 0.044715 * (x * x * x))))


def _sigmoid(x):
    return 1.0 / (1.0 + jnp.exp(-x))


def _merge_kernel(x_ref, ya_ref, ys_ref, xq_ref, km_ref, vm_ref, g1_ref, wg_ref, glu_w_ref, glu_b_ref,
                  wa_ref, ws_ref, wx_ref, wo_ref, o_ref):
    d = D_MODEL
    x = x_ref[...]
    h = _rmsnorm(x, g1_ref[...]).astype(BF16)

    mixed = _sigmoid(jnp.dot(h, wg_ref[:, 0:d], preferred_element_type=F32)) * jnp.dot(
        ya_ref[...], wa_ref[...], preferred_element_type=F32)

    z = _gelu_tanh(ys_ref[...])
    zb = z.astype(BF16)
    y_ssm = z * _sigmoid(jnp.dot(zb, glu_w_ref[...], preferred_element_type=F32) + glu_b_ref[...])
    mixed = mixed + _sigmoid(jnp.dot(h, wg_ref[:, d:2 * d], preferred_element_type=F32)) * jnp.dot(
        y_ssm.astype(BF16), ws_ref[...], preferred_element_type=F32)

    heads = []
    for hd in range(XA_HEADS):
        sl = slice(hd * XA_HEAD_DIM, (hd + 1) * XA_HEAD_DIM)
        s = lax.dot_general(xq_ref[:, sl], km_ref[0, :, sl], (((1,), (1,)), ((), ())),
                            preferred_element_type=F32) * (XA_HEAD_DIM ** -0.5)
        p = jnp.exp(s - jnp.max(s, axis=-1, keepdims=True))
        p = p * (1.0 / jnp.sum(p, axis=-1, keepdims=True))
        heads.append(jnp.dot(p.astype(BF16), vm_ref[0, :, sl], preferred_element_type=F32))
    y_x = jnp.concatenate(heads, axis=1).astype(BF16)
    mixed = mixed + _sigmoid(jnp.dot(h, wg_ref[:, 2 * d:3 * d], preferred_element_type=F32)) * jnp.dot(
        y_x, wx_ref[...], preferred_element_type=F32)

    o_ref[...] = x + jnp.dot(mixed.astype(BF16), wo_ref[...], preferred_element_type=F32)


def _merge(x2d, y_attn, y_s, xq, kmem, vmem, g1, wg, glu_w, glu_b, wa, ws, wx, wo, seq):
    n = x2d.shape[0]
    tiles_per_batch = seq // MERGE_TM
    m = kmem.shape[1]
    row = pl.BlockSpec((MERGE_TM, D_MODEL), lambda i: (i, 0))
    mem = pl.BlockSpec((1, m, D_MODEL), lambda i: (i // tiles_per_batch, 0, 0))
    sq = _const_spec((D_MODEL, D_MODEL))
    vec = _const_spec((1, D_MODEL))
    return pl.pallas_call(
        _merge_kernel,
        grid=(n // MERGE_TM,),
        in_specs=[row, row, row, row, mem, mem, vec, _const_spec((D_MODEL, 3 * D_MODEL)), sq, vec,
                  sq, sq, sq, sq],
        out_specs=row,
        out_shape=jax.ShapeDtypeStruct((n, D_MODEL), F32),
        compiler_params=_params(1),
        name="merge",
    )(x2d, y_attn, y_s, xq, kmem, vmem, g1, wg, glu_w, glu_b, wa, ws, wx, wo)


def _ffn_kernel(x_ref, g2_ref, wi_ref, wo_ref, gf_ref, o_ref):
    x = x_ref[...]
    h = _rmsnorm(x, g2_ref[...]).astype(BF16)
    acc = x
    for c in range(FFN_HIDDEN // FFN_CHUNK):
        lo = c * FFN_CHUNK
        gate = jnp.dot(h, wi_ref[:, lo:lo + FFN_CHUNK], preferred_element_type=F32)
        up = jnp.dot(h, wi_ref[:, FFN_HIDDEN + lo:FFN_HIDDEN + lo + FFN_CHUNK], preferred_element_type=F32)
        act = (gate * _sigmoid(gate) * up).astype(BF16)
        acc = acc + jnp.dot(act, wo_ref[lo:lo + FFN_CHUNK, :], preferred_element_type=F32)
    o_ref[...] = _rmsnorm(acc, gf_ref[...])


def _ffn(x2d, g2, wi, wo, gf):
    assert FFN_HIDDEN % FFN_CHUNK == 0
    n = x2d.shape[0]
    row = pl.BlockSpec((FFN_TM, D_MODEL), lambda i: (i, 0))
    vec = _const_spec((1, D_MODEL))
    return pl.pallas_call(
        _ffn_kernel,
        grid=(n // FFN_TM,),
        in_specs=[row, vec, _const_spec((D_MODEL, 2 * FFN_HIDDEN)), _const_spec((FFN_HIDDEN, D_MODEL)), vec],
        out_specs=row,
        out_shape=jax.ShapeDtypeStruct((n, D_MODEL), F32),
        compiler_params=_params(1),
        name="ffn",
    )(x2d, g2, wi, wo, gf)


def kernel(x, mem, norm1_g, w_in, da_lq1, da_lk1, da_lq2, da_lk2, da_subln_g, rel_bias, ssm_a_re, ssm_a_im, ssm_log_dt, ssm_b_re, ssm_b_im, ssm_c_re, ssm_c_im, ssm_d, glu_w, glu_b, mem_norm_g, w_mem_kv, w_br_attn, w_br_ssm, w_br_xattn, w_out, norm2_g, w_ffn_in, w_ffn_out, final_g):
    batch, seq, d = x.shape
    depth = w_in.shape[0]
    assert depth == 1 and d == D_MODEL and seq % PROJ_TM == 0
    layer = 0
    x2d = x.reshape(batch * seq, d)
    row = lambda v: v.reshape(1, -1).astype(F32)

    w = w_in[layer]
    scale = DA_HEAD_DIM ** -0.5 * LOG2E
    w4 = jnp.concatenate([w[:, 0:d] * scale, w[:, d:2 * d], w[:, 3 * d:4 * d], w[:, 4 * d:5 * d]],
                         axis=1).astype(BF16)
    wvt = jnp.transpose(w[:, 2 * d:3 * d]).astype(BF16)
    wg = w[:, 5 * d:8 * d].astype(BF16)

    bias, lam_tile = _prep_attn(rel_bias, row(da_lq1[layer]), row(da_lk1[layer]), row(da_lq2[layer]),
                                row(da_lk2[layer]))
    lam = lam_tile[0:1, 0:1]

    pow_re, pow_im, w_re, w_im = _prep_ssm(ssm_a_re[layer], ssm_a_im[layer], ssm_log_dt[layer],
                                           ssm_b_re[layer], ssm_b_im[layer])
    w_blk, c_blk, coef = _ssm_tables(pow_re, pow_im, w_re, w_im, ssm_c_re[layer], ssm_c_im[layer])

    q, k, u, xq, vt = _proj(x2d, row(norm1_g[layer]), w4, wvt, batch, seq)
    y_attn = _attention(lam, q.reshape(batch, seq, d), k.reshape(batch, seq, d), vt, bias,
                        da_subln_g[layer].reshape(DA_V_DIM, 1).astype(F32), batch, seq)
    y_s = _ssm(u.reshape(batch, seq, d), w_blk, c_blk, coef, row(ssm_d[layer]), batch, seq)
    kmem, vmem = _mem_kv(mem, row(mem_norm_g[layer]), w_mem_kv[layer].astype(BF16))
    x_mid = _merge(x2d, y_attn.reshape(batch * seq, d), y_s.reshape(batch * seq, d), xq, kmem, vmem,
                   row(norm1_g[layer]), wg, glu_w[layer].astype(BF16), row(glu_b[layer]),
                   w_br_attn[layer].astype(BF16), w_br_ssm[layer].astype(BF16), w_br_xattn[layer].astype(BF16),
                   w_out[layer].astype(BF16), seq)
    out = _ffn(x_mid, row(norm2_g[layer]), w_ffn_in[layer].astype(BF16), w_ffn_out[layer].astype(BF16),
               row(final_g))
    return out.reshape(batch, seq, d)
```

```python
import functools
import math

import jax
import jax.numpy as jnp
import numpy as np
from jax import lax
from jax.experimental import pallas as pl
from jax.experimental.pallas import tpu as pltpu

F32 = jnp.float32
BF16 = jnp.bfloat16

D_MODEL = 1024
CHUNK = 64
DA_HEADS = 8
DA_HEAD_DIM = 64
DA_V_DIM = 128
SSM_GROUP = 16
SSM_GROUPS = 64
SSM_STATE = 64
XA_HEADS = 4
XA_HEAD_DIM = 256
REL_BUCKETS = 32
REL_MAX_DIST = 256
FFN_HIDDEN = 2816
RMS_EPS = 1e-6
LAM_INIT = 0.8 - 0.6 * math.exp(-0.3 * 0)

LANES = 128
SUBLANES = 8
VMEM_LIMIT_BYTES = 56 * 1024 * 1024

ATT_BQ = 512
ATT_BK = 512
ATT_NEAR = ATT_BQ // ATT_BK + 1
ATT_V_ROWS = DA_V_DIM + 16
LOG2E = math.log2(math.e)
PROJ_TM = 1024
SSM_T = 4096
SSM_ROWS = 256
SSM_LANE_BLOCKS = D_MODEL // LANES
SSM_HALVES = 2
SSM_HALF_LANES = LANES // SSM_HALVES
SSM_GROUPS_PER_HALF = SSM_HALF_LANES // SSM_GROUP
SSM_HALF_STATES = SSM_GROUPS_PER_HALF * SSM_STATE
MERGE_TM = 512
FFN_TM = 1024
FFN_CHUNK = 256
MASK_VALUE = -1e30
MASK_BUCKET = REL_BUCKETS


def _params(n_axes, flags=None):
    return pltpu.CompilerParams(dimension_semantics=("arbitrary",) * n_axes,
                                vmem_limit_bytes=VMEM_LIMIT_BYTES, flags=flags)


def _const_spec(shape):
    nd = len(shape)
    return pl.BlockSpec(shape, lambda *_: (0,) * nd, pipeline_mode=pl.Buffered(1))


def _rmsnorm(xf, g):
    return xf * lax.rsqrt(jnp.mean(xf * xf, axis=-1, keepdims=True) + RMS_EPS) * g


def _t5_bucket_np(rel):
    half = REL_BUCKETS // 2
    max_exact = half // 2
    ret = np.where(rel > 0, half, 0)
    n = np.abs(rel)
    nf = np.maximum(n, 1).astype(np.float32)
    large = max_exact + (np.log(nf / np.float32(max_exact)) / np.float32(math.log(REL_MAX_DIST / max_exact))
                         * np.float32(half - max_exact)).astype(np.int32)
    large = np.minimum(large, half - 1)
    return (ret + np.where(n < max_exact, n, large)).astype(np.int32)


def _bucket_tiles():
    kk = np.arange(ATT_BK)[:, None]
    qq = np.arange(ATT_BQ)[None, :]
    tiles = []
    for n in range(ATT_NEAR):
        key = kk + (n - 1) * ATT_BK
        tile = _t5_bucket_np(key - qq)
        tiles.append(np.where((key // CHUNK) <= (qq // CHUNK), tile, MASK_BUCKET))
    nearest_far = (-2 * ATT_BK + ATT_BK - 1) - 0
    assert (_t5_bucket_np(np.arange(-8 * ATT_BQ, nearest_far + 1)) == REL_BUCKETS // 2 - 1).all()
    assert (tiles[-1][ATT_BK // 2:, :ATT_BQ // 2] == MASK_BUCKET).all()
    return np.stack(tiles).astype(np.int32)


def _prep_attn_kernel(present, relb_ref, bucket_ref, lq1_ref, lk1_ref, lq2_ref, lk2_ref, bias_ref, lam_ref):
    h = pl.program_id(0)
    for n, buckets in enumerate(present):
        bk = bucket_ref[n]
        acc = jnp.full(bk.shape, MASK_VALUE, F32)
        for b in buckets:
            acc = jnp.where(bk == b, relb_ref[b, h], acc)
        bias_ref[0, n] = (acc - relb_ref[REL_BUCKETS // 2 - 1, h]) * LOG2E
    lam = (jnp.exp(jnp.sum(lq1_ref[...] * lk1_ref[...], axis=-1, keepdims=True))
           - jnp.exp(jnp.sum(lq2_ref[...] * lk2_ref[...], axis=-1, keepdims=True)) + LAM_INIT)
    lam_ref[...] = jnp.broadcast_to(lam, lam_ref.shape)


def _prep_attn(rel_bias, lq1, lk1, lq2, lk2):
    tiles = _bucket_tiles()
    present = tuple(tuple(int(b) for b in np.unique(t) if b != MASK_BUCKET) for t in tiles)
    buckets = jnp.asarray(tiles)
    vec = pl.BlockSpec((1, DA_HEAD_DIM), lambda h: (0, 0))
    return pl.pallas_call(
        functools.partial(_prep_attn_kernel, present),
        grid=(DA_HEADS,),
        in_specs=[pl.BlockSpec(memory_space=pltpu.SMEM),
                  pl.BlockSpec((ATT_NEAR, ATT_BK, ATT_BQ), lambda h: (0, 0, 0)),
                  vec, vec, vec, vec],
        out_specs=[pl.BlockSpec((1, ATT_NEAR, ATT_BK, ATT_BQ), lambda h: (h, 0, 0, 0)),
                   pl.BlockSpec((SUBLANES, LANES), lambda h: (0, 0))],
        out_shape=[jax.ShapeDtypeStruct((DA_HEADS, ATT_NEAR, ATT_BK, ATT_BQ), F32),
                   jax.ShapeDtypeStruct((SUBLANES, LANES), F32)],
        compiler_params=_params(1),
        name="prep_attn",
    )(rel_bias, buckets, lq1, lk1, lq2, lk2)


def _prep_ssm_kernel(are_ref, aim_ref, ldt_ref, bre_ref, bim_ref, pr_ref, pi_ref, wr_ref, wi_ref):
    a_re = are_ref[...]
    a_im = aim_ref[...]
    dt = jnp.exp(ldt_ref[...])
    for n in range(1, SUBLANES + 1):
        mag = jnp.exp(a_re * dt * n)
        ang = a_im * dt * n
        pr_ref[n - 1] = mag * jnp.cos(ang)
        pi_ref[n - 1] = mag * jnp.sin(ang)
    xr = pr_ref[0] - 1.0
    xi = pi_ref[0]
    den = a_re * a_re + a_im * a_im
    cr = ((xr * a_re + xi * a_im) / den)[:, None, :]
    ci = ((xi * a_re - xr * a_im) / den)[:, None, :]
    b_re = bre_ref[...]
    b_im = bim_ref[...]
    bbr = cr * b_re - ci * b_im
    bbi = cr * b_im + ci * b_re
    wr_ref[0] = bbr
    wi_ref[0] = bbi
    for d in range(1, SUBLANES):
        ar = pr_ref[d - 1][:, None, :]
        ai = pi_ref[d - 1][:, None, :]
        wr_ref[d] = ar * bbr - ai * bbi
        wi_ref[d] = ar * bbi + ai * bbr


def _prep_ssm(a_re, a_im, log_dt, b_re, b_im):
    g, p = a_re.shape
    bt_re = jnp.transpose(b_re, (0, 2, 1))
    bt_im = jnp.transpose(b_im, (0, 2, 1))
    pow_shape = jax.ShapeDtypeStruct((SUBLANES, g, p), F32)
    w_shape = jax.ShapeDtypeStruct((SUBLANES, g, SSM_GROUP, p), F32)
    return pl.pallas_call(
        _prep_ssm_kernel,
        out_shape=[pow_shape, pow_shape, w_shape, w_shape],
        name="prep_ssm",
    )(a_re, a_im, log_dt.reshape(g, 1), bt_re, bt_im)


def _ssm_tables(pow_re, pow_im, w_re, w_im, c_re, c_im):
    nb, nh, gh = SSM_LANE_BLOCKS, SSM_HALVES, SSM_GROUPS_PER_HALF
    eye_g = jnp.eye(gh, dtype=bool)
    eye_h = jnp.eye(nh, dtype=bool)

    def w_block(w):
        w = jnp.transpose(w.reshape(SUBLANES, nb, nh, gh, SSM_GROUP, 1, SSM_STATE), (1, 2, 0, 3, 4, 5, 6))
        same_group = eye_g[None, None, None, :, None, :, None]
        return jnp.where(same_group, w, 0.0).reshape(nb, nh, SUBLANES * SSM_HALF_LANES, SSM_HALF_STATES)

    def c_block(cc):
        cc = jnp.transpose(cc.reshape(nb, nh, gh, SSM_GROUP, SSM_STATE), (0, 1, 2, 4, 3))
        cc = cc.reshape(nb, nh, gh, SSM_STATE, 1, 1, SSM_GROUP)
        same = (eye_h[None, :, None, None, :, None, None] & eye_g[None, None, :, None, None, :, None])
        return jnp.where(same, cc, 0.0).reshape(nb, nh, SSM_HALF_STATES, LANES)

    w_blk = jnp.concatenate([w_block(w_re), w_block(w_im)], axis=3).astype(BF16)
    c_blk = jnp.concatenate([c_block(c_re), c_block(-c_im)], axis=2).astype(BF16)

    def lanes(t):
        return jnp.transpose(t.reshape(SUBLANES, nb, nh, SSM_HALF_STATES), (1, 2, 0, 3))

    coef = jnp.stack([lanes(pow_re), lanes(pow_im)], axis=2)
    return w_blk, c_blk, coef


def _proj_kernel(x_ref, g_ref, w_ref, wvt_ref, q_ref, k_ref, u_ref, xq_ref, vt_ref):
    h = _rmsnorm(x_ref[...], g_ref[...]).astype(BF16)
    d = D_MODEL
    q_ref[...] = jnp.dot(h, w_ref[:, 0:d], preferred_element_type=F32).astype(BF16)
    k_ref[...] = jnp.dot(h, w_ref[:, d:2 * d], preferred_element_type=F32).astype(BF16)
    u_ref[...] = jnp.dot(h, w_ref[:, 2 * d:3 * d], preferred_element_type=F32)
    xq_ref[...] = jnp.dot(h, w_ref[:, 3 * d:4 * d], preferred_element_type=F32).astype(BF16)
    vt = lax.dot_general(wvt_ref[...], h, (((1,), (1,)), ((), ())), preferred_element_type=F32).astype(BF16)
    for hd in range(DA_HEADS):
        for jb in range(PROJ_TM // ATT_BK):
            vt_ref[0, hd, jb, 0:DA_V_DIM, :] = vt[hd * DA_V_DIM:(hd + 1) * DA_V_DIM,
                                                   jb * ATT_BK:(jb + 1) * ATT_BK]
            tail_row = lax.broadcasted_iota(jnp.int32, (ATT_V_ROWS - DA_V_DIM, ATT_BK), 0)
            vt_ref[0, hd, jb, DA_V_DIM:ATT_V_ROWS, :] = jnp.where(tail_row == 0, 1.0, 0.0).astype(BF16)


def _proj(x2d, g, w4, wvt, batch, seq):
    n = x2d.shape[0]
    tiles_per_batch = seq // PROJ_TM
    blocks_per_tile = PROJ_TM // ATT_BK
    row = pl.BlockSpec((PROJ_TM, D_MODEL), lambda i: (i, 0))
    act = lambda dt: jax.ShapeDtypeStruct((n, D_MODEL), dt)
    return pl.pallas_call(
        _proj_kernel,
        grid=(n // PROJ_TM,),
        in_specs=[row, _const_spec((1, D_MODEL)), _const_spec((D_MODEL, 4 * D_MODEL)),
                  _const_spec((D_MODEL, D_MODEL))],
        out_specs=[row, row, row, row,
                   pl.BlockSpec((1, DA_HEADS, blocks_per_tile, ATT_V_ROWS, ATT_BK),
                                lambda i: (i // tiles_per_batch, 0, i % tiles_per_batch, 0, 0))],
        out_shape=[act(BF16), act(BF16), act(F32), act(BF16),
                   jax.ShapeDtypeStruct((batch, DA_HEADS, seq // ATT_BK, ATT_V_ROWS, ATT_BK), BF16)],
        compiler_params=_params(1),
        name="proj",
    )(x2d, g, w4, wvt)


def _attn_kernel(lam_ref, q_ref, k_ref, vt_ref, bias_ref, g_ref, o_ref):
    ratio = ATT_BQ // ATT_BK
    lam = lam_ref[0, 0]
    gain = g_ref[...] * (1.0 - LAM_INIT)
    lane = lax.broadcasted_iota(jnp.int32, (ATT_BQ, 2 * DA_HEAD_DIM), 1)

    assert ratio == 1
    half = ATT_BK // 2
    nt = (((1,), (1,)), ((), ()))


    def scores(qc, j, tile, klen, cols):
        kb = k_ref[0, j * ATT_BK:j * ATT_BK + klen, :]
        out = []
        for c in range(2):
            s = lax.dot_general(kb, qc[c], nt, preferred_element_type=F32)
            if tile is not None:
                s = s + bias_ref[0, tile, 0:klen, cols]
            out.append((s, jnp.max(s, axis=0, keepdims=True)))
        return out

    def softmax(sc, m_prev):
        probs, m_next, alpha = [], [], []
        for c in range(2):
            s, m_new = sc[c]
            if m_prev is None:
                alpha.append(None)
            else:
                m_new = jnp.maximum(m_prev[c], m_new)
                alpha.append(jnp.exp2(m_prev[c] - m_new))
            m_next.append(m_new)
            probs.append(jnp.exp2(s - m_new).astype(BF16))
        return probs, m_next, alpha

    def values(j, probs, alpha, acc, klen):
        vtb = vt_ref[0, 0, j, :, 0:klen]
        out = []
        for c in range(2):
            pv = jnp.dot(vtb, probs[c], preferred_element_type=F32)
            out.append(pv if acc is None else alpha[c] * acc[c] + pv)
        return out

    def finish(row0, acc):
        num = [acc[c][0:DA_V_DIM, :] for c in range(2)]
        den = [acc[c][DA_V_DIM:DA_V_DIM + 1, :] for c in range(2)]
        o = num[0] * (1.0 / den[0]) - lam * (num[1] * (1.0 / den[1]))
        o = o * lax.rsqrt(jnp.mean(o * o, axis=0, keepdims=True) + RMS_EPS)
        o_ref[0, row0:row0 + half, :] = (o * gain).T.astype(BF16)

    for i in range(q_ref.shape[1] // ATT_BQ):
        qh = q_ref[0, i * ATT_BQ:(i + 1) * ATT_BQ, :]
        zero = jnp.zeros_like(qh)
        qs = (jnp.where(lane < DA_HEAD_DIM, qh, zero), jnp.where(lane >= DA_HEAD_DIM, qh, zero))
        first_near = ratio * i - 1
        last = first_near + ATT_NEAR - 1

        def tile_of(j, first_near=first_near):
            return j - first_near if 0 <= j - first_near < ATT_NEAR else None

        for hq in range(2):
            cols = slice(hq * half, (hq + 1) * half)
            qc = (qs[0][cols], qs[1][cols])

            def klen_of(j, last=last, hq=hq):
                return (hq + 1) * half if j == last else ATT_BK

            sc = scores(qc, 0, tile_of(0), klen_of(0), cols)
            m_run, acc, pending = None, None, None
            for t in range(last + 1):
                if pending is not None:
                    acc = values(t - 1, pending[0], pending[1], acc, klen_of(t - 1))
                probs, m_run, alpha = softmax(sc, m_run)
                pending = (probs, alpha)
                if t < last:
                    sc = scores(qc, t + 1, tile_of(t + 1), klen_of(t + 1), cols)
            acc = values(last, pending[0], pending[1], acc, klen_of(last))
            finish(i * ATT_BQ + hq * half, acc)


def _attention(lam, q, k, vt, bias, g_col, batch, seq):
    bq, bk = ATT_BQ, ATT_BK
    whole = pl.BlockSpec((1, seq, DA_V_DIM), lambda b, h: (b, 0, h))
    return pl.pallas_call(
        _attn_kernel,
        grid=(batch, DA_HEADS),
        in_specs=[pl.BlockSpec(memory_space=pltpu.SMEM),
                  whole,
                  whole,
                  pl.BlockSpec((1, 1, seq // bk, ATT_V_ROWS, bk), lambda b, h: (b, h, 0, 0, 0)),
                  pl.BlockSpec((1, ATT_NEAR, bk, bq), lambda b, h: (h, 0, 0, 0)),
                  pl.BlockSpec((DA_V_DIM, 1), lambda b, h: (0, 0))],
        out_specs=whole,
        out_shape=jax.ShapeDtypeStruct((batch, seq, D_MODEL), BF16),
        compiler_params=_params(2),
        name="attention",
    )(lam, q, k, vt, bias, g_col)


def _ssm_kernel(u_ref, w_ref, c_ref, coef_ref, d_ref, y_ref, h_sc):
    t = pl.program_id(2)
    ns = SSM_HALF_STATES

    @pl.when(t == 0)
    def _():
        h_sc[...] = jnp.zeros(h_sc.shape, F32)

    rows = SSM_ROWS
    groups = rows // SUBLANES
    halves = range(SSM_HALVES)

    def local_states(blk):
        u3 = u_ref[0, blk * rows:(blk + 1) * rows, :].reshape(groups, SUBLANES, LANES)
        row = lax.broadcasted_iota(jnp.int32, u3.shape, 1)
        low = lax.broadcasted_iota(jnp.int32, u3.shape, 2) < SSM_HALF_LANES
        delayed = [u3] + [jnp.where(row >= d, pltpu.roll(u3, d, axis=1), 0.0) for d in range(1, SUBLANES)]
        swapped = [pltpu.roll(ud, SSM_HALF_LANES, axis=2) for ud in delayed]
        out = []
        for h in halves:
            pieces = []
            for d in range(0, SUBLANES, 2):
                pair = (jnp.where(low, delayed[d], swapped[d + 1]) if h == 0
                        else jnp.where(low, swapped[d], delayed[d + 1]))
                pieces.append(pair.reshape(rows, LANES).astype(BF16))
            out.append(jnp.dot(jnp.concatenate(pieces, axis=1), w_ref[0, h], preferred_element_type=F32))
        return out

    def carry_and_project(blk, local, state):
        y = None
        new_state = []
        for h in halves:
            pr = coef_ref[0, h, 0]
            pi = coef_ref[0, h, 1]
            hr, hi = state[h]
            xr = local[h][:, :ns].reshape(groups, SUBLANES, ns)
            xi = local[h][:, ns:].reshape(groups, SUBLANES, ns)
            out_r, out_i = [], []
            for g in range(groups):
                gr = xr[g] + (pr * hr - pi * hi)
                gi = xi[g] + (pr * hi + pi * hr)
                hr = gr[SUBLANES - 1:SUBLANES]
                hi = gi[SUBLANES - 1:SUBLANES]
                out_r.append(gr)
                out_i.append(gi)
            new_state.append((hr, hi))
            x_all = jnp.concatenate([jnp.concatenate(out_r, axis=0), jnp.concatenate(out_i, axis=0)], axis=1)
            part = jnp.dot(x_all.astype(BF16), c_ref[0, h], preferred_element_type=F32)
            y = part if y is None else y + part
        sl = slice(blk * rows, (blk + 1) * rows)
        y_ref[0, sl, :] = y + d_ref[...] * u_ref[0, sl, :]
        return new_state

    n_blk = SSM_T // rows
    state = [(h_sc[h, 0], h_sc[h, 1]) for h in halves]
    pending = local_states(0)
    for blk in range(n_blk):
        upcoming = local_states(blk + 1) if blk + 1 < n_blk else None
        state = carry_and_project(blk, pending, state)
        pending = upcoming
    for h in halves:
        h_sc[h, 0] = state[h][0]
        h_sc[h, 1] = state[h][1]


def _ssm(u, w_blk, c_blk, coef, d_skip, batch, seq):
    nb = SSM_LANE_BLOCKS
    act = pl.BlockSpec((1, SSM_T, LANES), lambda b, j, t: (b, t, j))
    return pl.pallas_call(
        _ssm_kernel,
        grid=(batch, nb, seq // SSM_T),
        in_specs=[act,
                  pl.BlockSpec((1, SSM_HALVES, SUBLANES * SSM_HALF_LANES, 2 * SSM_HALF_STATES),
                               lambda b, j, t: (j, 0, 0, 0)),
                  pl.BlockSpec((1, SSM_HALVES, 2 * SSM_HALF_STATES, LANES), lambda b, j, t: (j, 0, 0, 0)),
                  pl.BlockSpec((1, SSM_HALVES, 2, SUBLANES, SSM_HALF_STATES), lambda b, j, t: (j, 0, 0, 0, 0)),
                  pl.BlockSpec((1, LANES), lambda b, j, t: (0, j))],
        out_specs=act,
        out_shape=jax.ShapeDtypeStruct((batch, seq, D_MODEL), F32),
        scratch_shapes=[pltpu.VMEM((SSM_HALVES, 2, 1, SSM_HALF_STATES), F32)],
        compiler_params=_params(3),
        name="ssm",
    )(u, w_blk, c_blk, coef, d_skip)


def _mem_kv_kernel(mem_ref, g_ref, w_ref, k_ref, v_ref):
    mn = _rmsnorm(mem_ref[0], g_ref[...]).astype(BF16)
    kv = jnp.dot(mn, w_ref[...], preferred_element_type=F32)
    k_ref[0] = kv[:, :D_MODEL].astype(BF16)
    v_ref[0] = kv[:, D_MODEL:].astype(BF16)


def _mem_kv(mem, g, w):
    batch, m, _ = mem.shape
    blk = pl.BlockSpec((1, m, D_MODEL), lambda b: (b, 0, 0))
    shape = jax.ShapeDtypeStruct((batch, m, D_MODEL), BF16)
    return pl.pallas_call(
        _mem_kv_kernel,
        grid=(batch,),
        in_specs=[blk, _const_spec((1, D_MODEL)), _const_spec((D_MODEL, 2 * D_MODEL))],
        out_specs=[blk, blk],
        out_shape=[shape, shape],
        compiler_params=_params(1),
        name="mem_kv",
    )(mem, g, w)


def _gelu_tanh(x):
    return 0.5 * x * (1.0 + jnp.tanh(math.sqrt(2.0 / math.pi) * (x + 0.044715 * (x * x * x))))


def _sigmoid(x):
    return 1.0 / (1.0 + jnp.exp(-x))


def _merge_kernel(x_ref, ya_ref, ys_ref, xq_ref, km_ref, vm_ref, g1_ref, wg_ref, glu_w_ref, glu_b_ref,
                  wa_ref, ws_ref, wx_ref, wo_ref, o_ref):
    d = D_MODEL
    x = x_ref[...]
    h = _rmsnorm(x, g1_ref[...]).astype(BF16)

    mixed = _sigmoid(jnp.dot(h, wg_ref[:, 0:d], preferred_element_type=F32)) * jnp.dot(
        ya_ref[...], wa_ref[...], preferred_element_type=F32)

    z = _gelu_tanh(ys_ref[...])
    zb = z.astype(BF16)
    y_ssm = z * _sigmoid(jnp.dot(zb, glu_w_ref[...], preferred_element_type=F32) + glu_b_ref[...])
    mixed = mixed + _sigmoid(jnp.dot(h, wg_ref[:, d:2 * d], preferred_element_type=F32)) * jnp.dot(
        y_ssm.astype(BF16), ws_ref[...], preferred_element_type=F32)

    heads = []
    for hd in range(XA_HEADS):
        sl = slice(hd * XA_HEAD_DIM, (hd + 1) * XA_HEAD_DIM)
        s = lax.dot_general(xq_ref[:, sl], km_ref[0, :, sl], (((1,), (1,)), ((), ())),
                            preferred_element_type=F32) * (XA_HEAD_DIM ** -0.5)
        p = jnp.exp(s - jnp.max(s, axis=-1, keepdims=True))
        p = p * (1.0 / jnp.sum(p, axis=-1, keepdims=True))
        heads.append(jnp.dot(p.astype(BF16), vm_ref[0, :, sl], preferred_element_type=F32))
    y_x = jnp.concatenate(heads, axis=1).astype(BF16)
    mixed = mixed + _sigmoid(jnp.dot(h, wg_ref[:, 2 * d:3 * d], preferred_element_type=F32)) * jnp.dot(
        y_x, wx_ref[...], preferred_element_type=F32)

    o_ref[...] = x + jnp.dot(mixed.astype(BF16), wo_ref[...], preferred_element_type=F32)


def _merge(x2d, y_attn, y_s, xq, kmem, vmem, g1, wg, glu_w, glu_b, wa, ws, wx, wo, seq):
    n = x2d.shape[0]
    tiles_per_batch = seq // MERGE_TM
    m = kmem.shape[1]
    row = pl.BlockSpec((MERGE_TM, D_MODEL), lambda i: (i, 0))
    mem = pl.BlockSpec((1, m, D_MODEL), lambda i: (i // tiles_per_batch, 0, 0))
    sq = _const_spec((D_MODEL, D_MODEL))
    vec = _const_spec((1, D_MODEL))
    return pl.pallas_call(
        _merge_kernel,
        grid=(n // MERGE_TM,),
        in_specs=[row, row, row, row, mem, mem, vec, _const_spec((D_MODEL, 3 * D_MODEL)), sq, vec,
                  sq, sq, sq, sq],
        out_specs=row,
        out_shape=jax.ShapeDtypeStruct((n, D_MODEL), F32),
        compiler_params=_params(1),
        name="merge",
    )(x2d, y_attn, y_s, xq, kmem, vmem, g1, wg, glu_w, glu_b, wa, ws, wx, wo)


def _ffn_kernel(x_ref, g2_ref, wi_ref, wo_ref, gf_ref, o_ref):
    x = x_ref[...]
    h = _rmsnorm(x, g2_ref[...]).astype(BF16)
    acc = x
    for c in range(FFN_HIDDEN // FFN_CHUNK):
        lo = c * FFN_CHUNK
        gate = jnp.dot(h, wi_ref[:, lo:lo + FFN_CHUNK], preferred_element_type=F32)
        up = jnp.dot(h, wi_ref[:, FFN_HIDDEN + lo:FFN_HIDDEN + lo + FFN_CHUNK], preferred_element_type=F32)
        act = (gate * _sigmoid(gate) * up).astype(BF16)
        acc = acc + jnp.dot(act, wo_ref[lo:lo + FFN_CHUNK, :], preferred_element_type=F32)
    o_ref[...] = _rmsnorm(acc, gf_ref[...])


def _ffn(x2d, g2, wi, wo, gf):
    assert FFN_HIDDEN % FFN_CHUNK == 0
    n = x2d.shape[0]
    row = pl.BlockSpec((FFN_TM, D_MODEL), lambda i: (i, 0))
    vec = _const_spec((1, D_MODEL))
    return pl.pallas_call(
        _ffn_kernel,
        grid=(n // FFN_TM,),
        in_specs=[row, vec, _const_spec((D_MODEL, 2 * FFN_HIDDEN)), _const_spec((FFN_HIDDEN, D_MODEL)), vec],
        out_specs=row,
        out_shape=jax.ShapeDtypeStruct((n, D_MODEL), F32),
        compiler_params=_params(1),
        name="ffn",
    )(x2d, g2, wi, wo, gf)


def kernel(x, mem, norm1_g, w_in, da_lq1, da_lk1, da_lq2, da_lk2, da_subln_g, rel_bias, ssm_a_re, ssm_a_im, ssm_log_dt, ssm_b_re, ssm_b_im, ssm_c_re, ssm_c_im, ssm_d, glu_w, glu_b, mem_norm_g, w_mem_kv, w_br_attn, w_br_ssm, w_br_xattn, w_out, norm2_g, w_ffn_in, w_ffn_out, final_g):
    batch, seq, d = x.shape
    depth = w_in.shape[0]
    assert depth == 1 and d == D_MODEL and seq % PROJ_TM == 0
    layer = 0
    x2d = x.reshape(batch * seq, d)
    row = lambda v: v.reshape(1, -1).astype(F32)

    w = w_in[layer]
    scale = DA_HEAD_DIM ** -0.5 * LOG2E
    w4 = jnp.concatenate([w[:, 0:d] * scale, w[:, d:2 * d], w[:, 3 * d:4 * d], w[:, 4 * d:5 * d]],
                         axis=1).astype(BF16)
    wvt = jnp.transpose(w[:, 2 * d:3 * d]).astype(BF16)
    wg = w[:, 5 * d:8 * d].astype(BF16)

    bias, lam_tile = _prep_attn(rel_bias, row(da_lq1[layer]), row(da_lk1[layer]), row(da_lq2[layer]),
                                row(da_lk2[layer]))
    lam = lam_tile[0:1, 0:1]

    pow_re, pow_im, w_re, w_im = _prep_ssm(ssm_a_re[layer], ssm_a_im[layer], ssm_log_dt[layer],
                                           ssm_b_re[layer], ssm_b_im[layer])
    w_blk, c_blk, coef = _ssm_tables(pow_re, pow_im, w_re, w_im, ssm_c_re[layer], ssm_c_im[layer])

    q, k, u, xq, vt = _proj(x2d, row(norm1_g[layer]), w4, wvt, batch, seq)
    y_attn = _attention(lam, q.reshape(batch, seq, d), k.reshape(batch, seq, d), vt, bias,
                        da_subln_g[layer].reshape(DA_V_DIM, 1).astype(F32), batch, seq)
    y_s = _ssm(u.reshape(batch, seq, d), w_blk, c_blk, coef, row(ssm_d[layer]), batch, seq)
    kmem, vmem = _mem_kv(mem, row(mem_norm_g[layer]), w_mem_kv[layer].astype(BF16))
    x_mid = _merge(x2d, y_attn.reshape(batch * seq, d), y_s.reshape(batch * seq, d), xq, kmem, vmem,
                   row(norm1_g[layer]), wg, glu_w[layer].astype(BF16), row(glu_b[layer]),
                   w_br_attn[layer].astype(BF16), w_br_ssm[layer].astype(BF16), w_br_xattn[layer].astype(BF16),
                   w_out[layer].astype(BF16), seq)
    out = _ffn(x_mid, row(norm2_g[layer]), w_ffn_in[layer].astype(BF16), w_ffn_out[layer].astype(BF16),
               row(final_g))
    return out.reshape(batch, seq, d)
```
